```python
import jax, jax.numpy as jnp
from jax import lax
import numpy as np

D_MODEL = 2048
BATCH = 2
SEQ = 8192
DEPTH = 1

MIX_WIDTH = D_MODEL
HEAD_DIM = 128
POOL_WINDOWS = (2, 4, 8, 16)
N_POOL_GROUPS = len(POOL_WINDOWS)
POOL_WIDTH = MIX_WIDTH // 4
POOL_GROUP_DIM = POOL_WIDTH // N_POOL_GROUPS
ATTN_WIDTH = MIX_WIDTH - POOL_WIDTH
N_ATTN_HEADS = ATTN_WIDTH // HEAD_DIM
DILATION_PAIRS = ((128, 1), (512, 4), (2048, 16))
N_DIL_GROUPS = len(DILATION_PAIRS)
HEADS_PER_GROUP = N_ATTN_HEADS // N_DIL_GROUPS
BLOCK = 128
D_FF = ((8 * D_MODEL // 3 + 255) // 256) * 256
D_PLE = 256
NUM_BUCKETS = 32
MAX_EXACT = NUM_BUCKETS // 2
MAX_DISTANCE = 2048
EPS = 1e-6
NEG_INF = -1e30
IN_PROJ_WIDTH = POOL_WIDTH + 3 * ATTN_WIDTH

kernel_name = "hymba_pool_dilated_attn_swiglu_ple"


def rms_norm(x, g):
    xf = x.astype(jnp.float32)
    y = xf * lax.rsqrt(jnp.mean(xf * xf, axis=-1, keepdims=True) + EPS)
    return (y * g.astype(jnp.float32)).astype(x.dtype)


def t5_causal_bucket(distance):
    n = jnp.maximum(distance, 0)
    nf = jnp.maximum(n, 1).astype(jnp.float32)
    large = MAX_EXACT + (jnp.log(nf / MAX_EXACT) / np.float32(np.log(MAX_DISTANCE / MAX_EXACT))
                         * (NUM_BUCKETS - MAX_EXACT)).astype(jnp.int32)
    large = jnp.minimum(large, NUM_BUCKETS - 1)
    return jnp.where(n < MAX_EXACT, n, large)


def pool_mixer(u, pool_w, pool_scale):
    B, S, _ = u.shape
    uf = u.astype(jnp.float32)
    cs = jnp.pad(jnp.cumsum(uf, axis=1), ((0, 0), (1, 0), (0, 0)))
    pos = jnp.arange(1, S + 1, dtype=jnp.int32)
    outs = []
    for gi, w in enumerate(POOL_WINDOWS):
        sl = slice(gi * POOL_GROUP_DIM, (gi + 1) * POOL_GROUP_DIM)
        c = cs[..., sl]
        prev = jnp.pad(c, ((0, 0), (w, 0), (0, 0)))[:, 1:S + 1]
        cnt = jnp.minimum(pos, w).astype(jnp.float32)[None, :, None]
        outs.append((c[:, 1:] - prev) / cnt - uf[..., sl])
    pooled = jnp.stack(outs, axis=2)
    mixed = jnp.einsum('bsgc,gcd->bsgd', pooled, pool_w.astype(jnp.float32))
    return (mixed.reshape(B, S, POOL_WIDTH) * pool_scale.astype(jnp.float32)).astype(u.dtype)


def dilated_window_attention(q, k, v, bias_table, window, dilation):
    B, S, H, Dh = q.shape
    d = dilation
    steps = window // d
    L = S // d
    nb = -(-L // BLOCK)
    Lp = nb * BLOCK

    def to_sub(t):
        t = t.reshape(B, L, d, H, Dh).astype(jnp.float32)
        return jnp.pad(t, ((0, 0), (0, Lp - L), (0, 0), (0, 0), (0, 0)))

    def band_keys(t):
        tp = jnp.pad(to_sub(t), ((0, 0), (BLOCK, 0), (0, 0), (0, 0), (0, 0)))
        prev = tp[:, :Lp].reshape(B, nb, BLOCK, d, H, Dh)
        cur = tp[:, BLOCK:].reshape(B, nb, BLOCK, d, H, Dh)
        return jnp.concatenate([prev, cur], axis=2)

    qs = to_sub(q).reshape(B, nb, BLOCK, d, H, Dh)
    kb = band_keys(k)
    vb = band_keys(v)
    scores = jnp.einsum('bnqrhe,bnkrhe->bnrhqk', qs, kb) * np.float32(Dh ** -0.5)

    qi = jnp.arange(BLOCK, dtype=jnp.int32)[:, None]
    ki = jnp.arange(2 * BLOCK, dtype=jnp.int32)[None, :]
    offset = qi + BLOCK - ki
    valid_off = (offset >= 0) & (offset <= steps)
    key_ok = (jnp.arange(nb)[:, None] > 0) | (ki >= BLOCK)
    mask = valid_off[None] & key_ok[:, None, :]
    bias = jnp.transpose(bias_table.astype(jnp.float32)[t5_causal_bucket(offset * d)], (2, 0, 1))

    logits = jnp.where(mask[None, :, None, None], scores + bias[None, None, None], NEG_INF)
    m = jnp.max(logits, axis=-1, keepdims=True)
    pexp = jnp.exp(logits - m)
    den = jnp.sum(pexp, axis=-1, keepdims=True)
    out = jnp.einsum('bnrhqk,bnkrhe->bnrhqe', pexp, vb) / den
    lse = (m + jnp.log(den))[..., 0]

    out = jnp.transpose(out, (0, 1, 4, 2, 3, 5)).reshape(B, Lp, d, H, Dh)[:, :L]
    lse = jnp.transpose(lse, (0, 1, 4, 2, 3)).reshape(B, Lp, d, H)[:, :L]
    return out.reshape(B, S, H, Dh), lse.reshape(B, S, H)


def dilated_mixture_attention(q, k, v, rel_bias):
    B, S, _ = q.shape
    q = q.reshape(B, S, N_ATTN_HEADS, HEAD_DIM)
    k = k.reshape(B, S, N_ATTN_HEADS, HEAD_DIM)
    v = v.reshape(B, S, N_ATTN_HEADS, HEAD_DIM)
    outs, lses = [], []
    for gi, (window, dil) in enumerate(DILATION_PAIRS):
        hs = slice(gi * HEADS_PER_GROUP, (gi + 1) * HEADS_PER_GROUP)
        o, l = dilated_window_attention(q[:, :, hs], k[:, :, hs], v[:, :, hs], rel_bias[:, hs], window, dil)
        outs.append(o)
        lses.append(l)
    outs = jnp.stack(outs, axis=2)
    alpha = jax.nn.softmax(jnp.stack(lses, axis=2), axis=2)
    return (outs * alpha[..., None]).reshape(B, S, ATTN_WIDTH).astype(q.dtype)


def setup_inputs(seed: int = 0) -> dict:
    key = jax.random.key(seed)
    ks = jax.random.split(key, 16)
    f32 = jnp.float32
    nrm = lambda k, shape, fan_in: jax.random.normal(k, shape, f32) * (fan_in ** -0.5)
    gain = lambda k, shape: 1.0 + 0.05 * jax.random.normal(k, shape, f32)
    return {
        "x": jax.random.normal(ks[0], (BATCH, SEQ, D_MODEL), f32),
        "p": jax.random.normal(ks[1], (DEPTH, BATCH, SEQ, D_PLE), f32),
        "rel_bias": 0.5 * jax.random.normal(ks[2], (NUM_BUCKETS, N_ATTN_HEADS), f32),
        "norm_mix_g": gain(ks[3], (DEPTH, D_MODEL)),
        "w_in": nrm(ks[4], (DEPTH, D_MODEL, IN_PROJ_WIDTH), D_MODEL),
        "pool_w": nrm(ks[5], (DEPTH, N_POOL_GROUPS, POOL_GROUP_DIM, POOL_GROUP_DIM), POOL_GROUP_DIM),
        "pool_scale": gain(ks[6], (DEPTH, POOL_WIDTH)),
        "w_out": nrm(ks[7], (DEPTH, MIX_WIDTH, D_MODEL), MIX_WIDTH),
        "norm_ffn_g": gain(ks[8], (DEPTH, D_MODEL)),
        "w_gate": nrm(ks[9], (DEPTH, D_MODEL, D_FF), D_MODEL),
        "w_up": nrm(ks[10], (DEPTH, D_MODEL, D_FF), D_MODEL),
        "w_down": nrm(ks[11], (DEPTH, D_FF, D_MODEL), D_FF),
        "norm_ple_g": gain(ks[12], (DEPTH, D_MODEL)),
        "w_ple_gate": nrm(ks[13], (DEPTH, D_MODEL, D_MODEL), D_MODEL),
        "w_ple_proj": nrm(ks[14], (DEPTH, D_PLE, D_MODEL), D_PLE),
        "final_norm_g": gain(ks[15], (D_MODEL,)),
    }


def reference(x, p, rel_bias, norm_mix_g, w_in, pool_w, pool_scale, w_out,
              norm_ffn_g, w_gate, w_up, w_down, norm_ple_g, w_ple_gate, w_ple_proj,
              final_norm_g):
    h = x
    for i in range(DEPTH):
        xn = rms_norm(h, norm_mix_g[i])
        proj = xn @ w_in[i]
        u_pool = proj[..., :POOL_WIDTH]
        q = proj[..., POOL_WIDTH:POOL_WIDTH + ATTN_WIDTH]
        k = proj[..., POOL_WIDTH + ATTN_WIDTH:POOL_WIDTH + 2 * ATTN_WIDTH]
        v = proj[..., POOL_WIDTH + 2 * ATTN_WIDTH:]
        y_pool = pool_mixer(u_pool, pool_w[i], pool_scale[i])
        y_attn = dilated_mixture_attention(q, k, v, rel_bias)
        h = h + jnp.concatenate([y_pool, y_attn], axis=-1) @ w_out[i]
        hn = rms_norm(h, norm_ffn_g[i])
        h = h + (jax.nn.silu(hn @ w_gate[i]) * (hn @ w_up[i])) @ w_down[i]
        gate = jax.nn.sigmoid(rms_norm(h, norm_ple_g[i]) @ w_ple_gate[i])
        h = h + gate * (p[i] @ w_ple_proj[i])
    return rms_norm(h, final_norm_g)
```

```python
import functools

import jax
import jax.numpy as jnp
import numpy as np
from jax import lax
from jax.experimental import pallas as pl
from jax.experimental.pallas import tpu as pltpu

F32 = jnp.float32
BF16 = jnp.bfloat16

D_MODEL = 2048
POOL_WINDOWS = (2, 4, 8, 16)
POOL_WIDTH = 512
POOL_GROUP_DIM = 128
ATTN_WIDTH = 1536
HEAD_DIM = 128
DILATION_PAIRS = ((128, 1), (512, 4), (2048, 16))
HEADS_PER_GROUP = 4
GROUP_WIDTH = HEADS_PER_GROUP * HEAD_DIM
BLOCK = 128
D_FF = 5632
D_PLE = 256
NUM_BUCKETS = 32
MAX_EXACT = 16
MAX_DISTANCE = 2048
EPS = 1e-6
NEG_INF = -1e30
IN_PROJ_WIDTH = POOL_WIDTH + 3 * ATTN_WIDTH
QKV_WIDTH = 3 * ATTN_WIDTH
LSE_LANES = 128 // HEADS_PER_GROUP
POOL_HALO = 16

MIB = 1024 * 1024

TM_IN, TN_IN = 1024, 512
TM_MIX = 512
TM_FFN, TF_FFN = 512, 512
TM_PLE = 512


def _rms_scale(x):
    return lax.rsqrt(jnp.mean(x * x, axis=-1, keepdims=True) + EPS)


def _in_proj_kernel(x_ref, g_ref, w_ref, u_ref, qkv_ref, xn_ref):
    j = pl.program_id(1)

    @pl.when(j == 0)
    def _():
        x = x_ref[...]
        xn_ref[...] = (x * _rms_scale(x) * g_ref[...]).astype(BF16)

    acc = jnp.dot(xn_ref[...], w_ref[...], preferred_element_type=F32)

    @pl.when(j == 0)
    def _():
        u_ref[...] = acc

    @pl.when(j > 0)
    def _():
        qkv_ref[...] = acc.astype(BF16)


def _in_proj(x2, g, w_in):
    T = x2.shape[0]
    grid = (T // TM_IN, IN_PROJ_WIDTH // TN_IN)
    return pl.pallas_call(
        _in_proj_kernel,
        grid=grid,
        in_specs=[
            pl.BlockSpec((TM_IN, D_MODEL), lambda i, j: (i, 0)),
            pl.BlockSpec((1, D_MODEL), lambda i, j: (0, 0)),
            pl.BlockSpec((D_MODEL, TN_IN), lambda i, j: (0, j)),
        ],
        out_specs=[
            pl.BlockSpec((TM_IN, POOL_WIDTH), lambda i, j: (i, 0)),
            pl.BlockSpec((TM_IN, TN_IN), lambda i, j: (i, jnp.maximum(j - 1, 0))),
        ],
        out_shape=[
            jax.ShapeDtypeStruct((T, POOL_WIDTH), F32),
            jax.ShapeDtypeStruct((T, QKV_WIDTH), BF16),
        ],
        scratch_shapes=[pltpu.VMEM((TM_IN, D_MODEL), BF16)],
        compiler_params=pltpu.CompilerParams(
            dimension_semantics=("arbitrary", "arbitrary"),
            vmem_limit_bytes=48 * MIB),
        name="in_proj",
    )(x2, g, w_in)


def _attn_kernel(rb_ref, idx_ref, q_ref, kp_ref, kc_ref, vp_ref, vc_ref,
                 o_ref, lse_ref, bias_ref):
    b, r, n = pl.program_id(0), pl.program_id(1), pl.program_id(2)

    @pl.when((b == 0) & (r == 0) & (n == 0))
    def _():
        idx = idx_ref[...]
        for h in range(HEADS_PER_GROUP):
            acc = jnp.zeros((BLOCK, 2 * BLOCK), F32)
            for bkt in range(NUM_BUCKETS):
                acc = jnp.where(idx == bkt, rb_ref[bkt, h], acc)
            bias_ref[h] = acc

    qi = lax.broadcasted_iota(jnp.int32, (BLOCK, 2 * BLOCK), 0)
    ki = lax.broadcasted_iota(jnp.int32, (BLOCK, 2 * BLOCK), 1)
    off = qi + BLOCK - ki
    valid = (off >= 0) & (off <= BLOCK) & ((n > 0) | (ki >= BLOCK))
    lane = lax.broadcasted_iota(jnp.int32, (BLOCK, 128), 1)
    scale = np.float32(HEAD_DIM ** -0.5)

    lse_packed = jnp.zeros((BLOCK, 128), F32)
    for h in range(HEADS_PER_GROUP):
        hs = slice(h * HEAD_DIM, (h + 1) * HEAD_DIM)
        k = jnp.concatenate([kp_ref[:, hs], kc_ref[:, hs]], axis=0)
        v = jnp.concatenate([vp_ref[:, hs], vc_ref[:, hs]], axis=0)
        s = lax.dot_general(q_ref[:, hs], k, (((1,), (1,)), ((), ())),
                            preferred_element_type=F32)
        logits = jnp.where(valid, s * scale + bias_ref[h], NEG_INF)
        m = jnp.max(logits, axis=-1, keepdims=True)
        p = jnp.exp(logits - m)
        den = jnp.sum(p, axis=-1, keepdims=True)
        pv = jnp.dot(p.astype(BF16), v, preferred_element_type=F32)
        o_ref[:, hs] = (pv / den).astype(BF16)
        lse = m + jnp.log(den)
        lse_packed = jnp.where((lane >= h * LSE_LANES) & (lane < (h + 1) * LSE_LANES),
                               lse, lse_packed)
    lse_ref[...] = lse_packed


def _attention_group(qkv, rel_bias_g, gi, batch, seq):
    window, d = DILATION_PAIRS[gi]
    steps = window // d
    assert steps == BLOCK
    L = seq // d
    assert L % BLOCK == 0
    nb = L // BLOCK
    T = batch * seq
    ncol = QKV_WIDTH // GROUP_WIDTH
    q_col, k_col, v_col = gi, 3 + gi, 6 + gi

    qkv_v = qkv.reshape(batch * L, d * QKV_WIDTH)

    qi = jnp.arange(BLOCK, dtype=jnp.int32)[:, None]
    ki = jnp.arange(2 * BLOCK, dtype=jnp.int32)[None, :]
    dist = jnp.maximum((qi + BLOCK - ki) * d, 0)
    nf = jnp.maximum(dist, 1).astype(F32)
    large = MAX_EXACT + (jnp.log(nf / MAX_EXACT) / np.float32(np.log(MAX_DISTANCE / MAX_EXACT))
                         * (NUM_BUCKETS - MAX_EXACT)).astype(jnp.int32)
    large = jnp.minimum(large, NUM_BUCKETS - 1)
    bucket = jnp.where(dist < MAX_EXACT, dist, large)

    def cur(col):
        return lambda b, r, n: (b * nb + n, r * ncol + col)

    def prev(col):
        return lambda b, r, n: (b * nb + jnp.maximum(n - 1, 0), r * ncol + col)

    blk = (BLOCK, GROUP_WIDTH)
    out, lse = pl.pallas_call(
        _attn_kernel,
        grid=(batch, d, nb),
        in_specs=[
            pl.BlockSpec(memory_space=pltpu.SMEM),
            pl.BlockSpec((BLOCK, 2 * BLOCK), lambda b, r, n: (0, 0)),
            pl.BlockSpec(blk, cur(q_col)),
            pl.BlockSpec(blk, prev(k_col)),
            pl.BlockSpec(blk, cur(k_col)),
            pl.BlockSpec(blk, prev(v_col)),
            pl.BlockSpec(blk, cur(v_col)),
        ],
        out_specs=[
            pl.BlockSpec(blk, lambda b, r, n: (b * nb + n, r)),
            pl.BlockSpec((BLOCK, 128), lambda b, r, n: (b * nb + n, r)),
        ],
        out_shape=[
            jax.ShapeDtypeStruct((batch * L, d * GROUP_WIDTH), BF16),
            jax.ShapeDtypeStruct((batch * L, d * 128), F32),
        ],
        scratch_shapes=[pltpu.VMEM((HEADS_PER_GROUP, BLOCK, 2 * BLOCK), F32)],
        compiler_params=pltpu.CompilerParams(
            dimension_semantics=("arbitrary", "arbitrary", "arbitrary")),
        name=f"attn_d{d}",
    )(rel_bias_g, bucket, qkv_v, qkv_v, qkv_v, qkv_v, qkv_v)
    return out.reshape(T, GROUP_WIDTH), lse.reshape(T, 128)


def _mix_out_kernel(tiles_per_seq,
                    u_ref, halo_ref, pw_ref, ps_ref,
                    o1_ref, o2_ref, o3_ref, l1_ref, l2_ref, l3_ref,
                    wo_ref, x_ref, g_ref,
                    h_ref, hn_ref, ext_ref, a_ref):
    i = pl.program_id(0)
    tm = u_ref.shape[0]
    seq_tile = i % tiles_per_seq

    halo = halo_ref[...]
    ext_ref[0:POOL_HALO, :] = jnp.where(seq_tile > 0, halo, jnp.zeros_like(halo))
    ext_ref[POOL_HALO:, :] = u_ref[...]
    pos = seq_tile * tm + lax.broadcasted_iota(jnp.int32, (tm, 1), 0)
    for gi, w in enumerate(POOL_WINDOWS):
        cs = slice(gi * POOL_GROUP_DIM, (gi + 1) * POOL_GROUP_DIM)
        cur = ext_ref[POOL_HALO:, cs]
        tot = cur
        for k in range(1, w):
            tot = tot + ext_ref[POOL_HALO - k:POOL_HALO - k + tm, cs]
        cnt = jnp.minimum(pos + 1, w).astype(F32)
        pooled = tot / cnt - cur
        mixed = jnp.dot(pooled.astype(BF16), pw_ref[gi], preferred_element_type=F32)
        a_ref[:, cs] = (mixed * ps_ref[:, cs]).astype(BF16)

    l1, l2, l3 = l1_ref[...], l2_ref[...], l3_ref[...]
    m = jnp.maximum(jnp.maximum(l1, l2), l3)
    e1, e2, e3 = jnp.exp(l1 - m), jnp.exp(l2 - m), jnp.exp(l3 - m)
    tot_e = e1 + e2 + e3
    for gi, (o_ref, e) in enumerate(((o1_ref, e1), (o2_ref, e2), (o3_ref, e3))):
        alpha = e / tot_e
        for h in range(HEADS_PER_GROUP):
            a_h = alpha[:, h * LSE_LANES:h * LSE_LANES + 1]
            hs = slice(h * HEAD_DIM, (h + 1) * HEAD_DIM)
            col = POOL_WIDTH + gi * GROUP_WIDTH + h * HEAD_DIM
            a_ref[:, col:col + HEAD_DIM] = (o_ref[:, hs].astype(F32) * a_h).astype(BF16)

    h = x_ref[...] + jnp.dot(a_ref[...], wo_ref[...], preferred_element_type=F32)
    h_ref[...] = h
    hn_ref[...] = (h * _rms_scale(h) * g_ref[...]).astype(BF16)


def _mix_out(u, pool_w, pool_scale, outs, lses, w_out, x2, g_ffn, seq):
    T = x2.shape[0]
    tm = TM_MIX
    assert seq % tm == 0 and tm % POOL_HALO == 0
    row = lambda w: pl.BlockSpec((tm, w), lambda i: (i, 0))
    const = lambda shape: pl.BlockSpec(shape, lambda i: (0,) * len(shape))
    halo_blocks = tm // POOL_HALO
    return pl.pallas_call(
        functools.partial(_mix_out_kernel, seq // tm),
        grid=(T // tm,),
        in_specs=[
            row(POOL_WIDTH),
            pl.BlockSpec((POOL_HALO, POOL_WIDTH),
                         lambda i: (jnp.maximum(i * halo_blocks - 1, 0), 0)),
            const((len(POOL_WINDOWS), POOL_GROUP_DIM, POOL_GROUP_DIM)),
            const((1, POOL_WIDTH)),
            row(GROUP_WIDTH), row(GROUP_WIDTH), row(GROUP_WIDTH),
            row(128), row(128), row(128),
            pl.BlockSpec((D_MODEL, D_MODEL), lambda i: (0, 0), pipeline_mode=pl.Buffered(1)),
            row(D_MODEL),
            const((1, D_MODEL)),
        ],
        out_specs=[row(D_MODEL), row(D_MODEL)],
        out_shape=[
            jax.ShapeDtypeStruct((T, D_MODEL), F32),
            jax.ShapeDtypeStruct((T, D_MODEL), BF16),
        ],
        scratch_shapes=[
            pltpu.VMEM((POOL_HALO + tm, POOL_WIDTH), F32),
            pltpu.VMEM((tm, D_MODEL), BF16),
        ],
        compiler_params=pltpu.CompilerParams(
            dimension_semantics=("arbitrary",),
            vmem_limit_bytes=52 * MIB),
        name="mix_out",
    )(u, u, pool_w, pool_scale, *outs, *lses, w_out, x2, g_ffn)


def _ffn_kernel(hn_ref, h_ref, wg_ref, wu_ref, wd_ref, o_ref):
    f = pl.program_id(1)

    @pl.when(f == 0)
    def _():
        o_ref[...] = h_ref[...]

    hn = hn_ref[...]
    g = jnp.dot(hn, wg_ref[...], preferred_element_type=F32)
    u = jnp.dot(hn, wu_ref[...], preferred_element_type=F32)
    a = (g * jax.nn.sigmoid(g) * u).astype(BF16)
    o_ref[...] += jnp.dot(a, wd_ref[...], preferred_element_type=F32)


def _ffn(hn, h1, w_gate, w_up, w_down):
    T = hn.shape[0]
    tm, tf = TM_FFN, TF_FFN
    return pl.pallas_call(
        _ffn_kernel,
        grid=(T // tm, D_FF // tf),
        in_specs=[
            pl.BlockSpec((tm, D_MODEL), lambda i, f: (i, 0)),
            pl.BlockSpec((tm, D_MODEL), lambda i, f: (i, 0)),
            pl.BlockSpec((D_MODEL, tf), lambda i, f: (0, f)),
            pl.BlockSpec((D_MODEL, tf), lambda i, f: (0, f)),
            pl.BlockSpec((tf, D_MODEL), lambda i, f: (f, 0)),
        ],
        out_specs=pl.BlockSpec((tm, D_MODEL), lambda i, f: (i, 0)),
        out_shape=jax.ShapeDtypeStruct((T, D_MODEL), F32),
        compiler_params=pltpu.CompilerParams(
            dimension_semantics=("arbitrary", "arbitrary"),
            vmem_limit_bytes=48 * MIB),
        name="ffn",
    )(hn, h1, w_gate, w_up, w_down)


def _ple_kernel(h_ref, p_ref, gp_ref, wpg_ref, wpp_ref, gf_ref, y_ref):
    h = h_ref[...]
    hn = (h * _rms_scale(h) * gp_ref[...]).astype(BF16)
    gate = jax.nn.sigmoid(jnp.dot(hn, wpg_ref[...], preferred_element_type=F32))
    emb = jnp.dot(p_ref[...].astype(BF16), wpp_ref[...], preferred_element_type=F32)
    h = h + gate * emb
    y_ref[...] = h * _rms_scale(h) * gf_ref[...]


def _ple_final(h2, p2, g_ple, w_pg, w_pp, g_final):
    T = h2.shape[0]
    tm = TM_PLE
    return pl.pallas_call(
        _ple_kernel,
        grid=(T // tm,),
        in_specs=[
            pl.BlockSpec((tm, D_MODEL), lambda i: (i, 0)),
            pl.BlockSpec((tm, D_PLE), lambda i: (i, 0)),
            pl.BlockSpec((1, D_MODEL), lambda i: (0, 0)),
            pl.BlockSpec((D_MODEL, D_MODEL), lambda i: (0, 0), pipeline_mode=pl.Buffered(1)),
            pl.BlockSpec((D_PLE, D_MODEL), lambda i: (0, 0)),
            pl.BlockSpec((1, D_MODEL), lambda i: (0, 0)),
        ],
        out_specs=pl.BlockSpec((tm, D_MODEL), lambda i: (i, 0)),
        out_shape=jax.ShapeDtypeStruct((T, D_MODEL), F32),
        compiler_params=pltpu.CompilerParams(
            dimension_semantics=("arbitrary",),
            vmem_limit_bytes=48 * MIB),
        name="ple_final",
    )(h2, p2, g_ple, w_pg, w_pp, g_final)


def kernel(x, p, rel_bias, norm_mix_g, w_in, pool_w, pool_scale, w_out, norm_ffn_g,
           w_gate, w_up, w_down, norm_ple_g, w_ple_gate, w_ple_proj, final_norm_g):
    B, S, D = x.shape
    depth = w_in.shape[0]
    T = B * S
    h = x.reshape(T, D)
    row = lambda v: v.reshape(1, -1).astype(F32)
    for i in range(depth):
        u, qkv = _in_proj(h, row(norm_mix_g[i]), w_in[i].astype(BF16))
        outs, lses = [], []
        for gi in range(len(DILATION_PAIRS)):
            hs = slice(gi * HEADS_PER_GROUP, (gi + 1) * HEADS_PER_GROUP)
            o, l = _attention_group(qkv, rel_bias[:, hs].astype(F32), gi, B, S)
            outs.append(o)
            lses.append(l)
        h1, hn = _mix_out(u, pool_w[i].astype(BF16), row(pool_scale[i]), outs, lses,
                          w_out[i].astype(BF16), h, row(norm_ffn_g[i]), S)
        h2 = _ffn(hn, h1, w_gate[i].astype(BF16), w_up[i].astype(BF16), w_down[i].astype(BF16))
        assert depth == 1
        h = _ple_final(h2, p[i].reshape(T, D_PLE), row(norm_ple_g[i]),
                       w_ple_gate[i].astype(BF16), w_ple_proj[i].astype(BF16), row(final_norm_g))
    return h.reshape(B, S, D)
```

```python
import functools

import jax
import jax.numpy as jnp
import numpy as np
from jax import lax
from jax.experimental import pallas as pl
from jax.experimental.pallas import tpu as pltpu

F32 = jnp.float32
BF16 = jnp.bfloat16

D_MODEL = 2048
POOL_WINDOWS = (2, 4, 8, 16)
POOL_WIDTH = 512
POOL_GROUP_DIM = 128
ATTN_WIDTH = 1536
HEAD_DIM = 128
DILATION_PAIRS = ((128, 1), (512, 4), (2048, 16))
HEADS_PER_GROUP = 4
GROUP_WIDTH = HEADS_PER_GROUP * HEAD_DIM
BLOCK = 128
D_FF = 5632
D_PLE = 256
NUM_BUCKETS = 32
MAX_EXACT = 16
MAX_DISTANCE = 2048
EPS = 1e-6
NEG_INF = -1e30
IN_PROJ_WIDTH = POOL_WIDTH + 3 * ATTN_WIDTH
QKV_WIDTH = 3 * ATTN_WIDTH
LSE_LANES = 128 // HEADS_PER_GROUP
POOL_HALO = 16
PERM_ROWS = 256

MIB = 1024 * 1024

TM_IN, TN_IN = 1024, 512
TM_ATTN_LOCAL = 512
TM_MIX = 512
TM_FFN, TF_FFN = 512, 512
TM_PLE = 512


def _rms_scale(x):
    return lax.rsqrt(jnp.mean(x * x, axis=-1, keepdims=True) + EPS)


def _in_proj_kernel(x_ref, g_ref, w_ref, u_ref, qkv_ref, xn_ref):
    j = pl.program_id(1)

    @pl.when(j == 0)
    def _():
        x = x_ref[...]
        xn_ref[...] = (x * _rms_scale(x) * g_ref[...]).astype(BF16)

    acc = jnp.dot(xn_ref[...], w_ref[...], preferred_element_type=F32)

    @pl.when(j == 0)
    def _():
        u_ref[...] = acc

    @pl.when(j > 0)
    def _():
        qkv_ref[...] = acc.astype(BF16)


def _in_proj(x2, g, w_in):
    T = x2.shape[0]
    grid = (T // TM_IN, IN_PROJ_WIDTH // TN_IN)
    return pl.pallas_call(
        _in_proj_kernel,
        grid=grid,
        in_specs=[
            pl.BlockSpec((TM_IN, D_MODEL), lambda i, j: (i, 0)),
            pl.BlockSpec((1, D_MODEL), lambda i, j: (0, 0)),
            pl.BlockSpec((D_MODEL, TN_IN), lambda i, j: (0, j)),
        ],
        out_specs=[
            pl.BlockSpec((TM_IN, POOL_WIDTH), lambda i, j: (i, 0)),
            pl.BlockSpec((TM_IN, TN_IN), lambda i, j: (i, jnp.maximum(j - 1, 0))),
        ],
        out_shape=[
            jax.ShapeDtypeStruct((T, POOL_WIDTH), F32),
            jax.ShapeDtypeStruct((T, QKV_WIDTH), BF16),
        ],
        scratch_shapes=[pltpu.VMEM((TM_IN, D_MODEL), BF16)],
        compiler_params=pltpu.CompilerParams(
            dimension_semantics=("arbitrary", "arbitrary"),
            vmem_limit_bytes=48 * MIB),
        name="in_proj",
    )(x2, g, w_in)


def _build_bias(rb_ref, idx_ref, bias_ref):
    idx = idx_ref[...]
    qi = lax.broadcasted_iota(jnp.int32, (BLOCK, 2 * BLOCK), 0)
    ki = lax.broadcasted_iota(jnp.int32, (BLOCK, 2 * BLOCK), 1)
    off = qi + BLOCK - ki
    band = (off >= 0) & (off <= BLOCK)
    for h in range(HEADS_PER_GROUP):
        acc = jnp.zeros((BLOCK, 2 * BLOCK), F32)
        for bkt in range(NUM_BUCKETS):
            acc = jnp.where(idx == bkt, rb_ref[bkt, h], acc)
        bias_ref[0, h] = jnp.where(band, acc, NEG_INF)
        bias_ref[1, h] = jnp.where(band & (ki >= BLOCK), acc, NEG_INF)


def _attn_unit(q_ref, kp_ref, kc_ref, vp_ref, vc_ref, bias_ref, first):
    lane = lax.broadcasted_iota(jnp.int32, (BLOCK, 128), 1)
    scale = np.float32(HEAD_DIM ** -0.5)
    outs = []
    lse_packed = jnp.zeros((BLOCK, 128), F32)
    for h in range(HEADS_PER_GROUP):
        hs = slice(h * HEAD_DIM, (h + 1) * HEAD_DIM)
        k = jnp.concatenate([kp_ref[:, hs], kc_ref[:, hs]], axis=0)
        v = jnp.concatenate([vp_ref[:, hs], vc_ref[:, hs]], axis=0)
        s = lax.dot_general(q_ref[:, hs], k, (((1,), (1,)), ((), ())),
                            preferred_element_type=F32)
        mb = bias_ref[first, h]
        logits = jnp.where(mb > 0.5 * NEG_INF, s * scale + mb, NEG_INF)
        m = jnp.max(logits, axis=-1, keepdims=True)
        p = jnp.exp(logits - m)
        den = jnp.sum(p, axis=-1, keepdims=True)
        pv = jnp.dot(p.astype(BF16), v, preferred_element_type=F32)
        outs.append((pv / den).astype(BF16))
        lse = m + jnp.log(den)
        lse_packed = jnp.where((lane >= h * LSE_LANES) & (lane < (h + 1) * LSE_LANES),
                               lse, lse_packed)
    return jnp.concatenate(outs, axis=1), lse_packed


def _attn_local_kernel(rb_ref, idx_ref, q_ref, k_ref, kh_ref, v_ref, vh_ref,
                       o_ref, lse_ref, bias_ref):
    b, n = pl.program_id(0), pl.program_id(1)

    @pl.when((b == 0) & (n == 0))
    def _():
        _build_bias(rb_ref, idx_ref, bias_ref)

    first = (n == 0).astype(jnp.int32)
    for i in range(q_ref.shape[0] // BLOCK):
        rows = pl.ds(i * BLOCK, BLOCK)
        if i == 0:
            kp, vp, f = kh_ref, vh_ref, first
        else:
            prev_rows = pl.ds((i - 1) * BLOCK, BLOCK)
            kp, vp, f = k_ref.at[prev_rows], v_ref.at[prev_rows], 0
        out, lse = _attn_unit(q_ref.at[rows], kp, k_ref.at[rows], vp, v_ref.at[rows], bias_ref, f)
        o_ref[rows, :] = out
        lse_ref[rows, :] = lse


def _attn_dilated_kernel(d, rb_ref, idx_ref, pm_ref, pmt_ref, q_ref, k_ref, v_ref,
                         o_ref, lse_ref, bias_ref, qp_ref, kp_ref, vp_ref, op_ref, lp_ref):
    b, n = pl.program_id(0), pl.program_id(1)
    nblk = q_ref.shape[0] // PERM_ROWS
    chunk = PERM_ROWS // d
    slot = n % 2
    pslot = 1 - slot

    @pl.when((b == 0) & (n == 0))
    def _():
        _build_bias(rb_ref, idx_ref, bias_ref)

    @pl.when(n == 0)
    def _():
        kp_ref[pslot] = jnp.zeros(kp_ref.shape[1:], BF16)
        vp_ref[pslot] = jnp.zeros(vp_ref.shape[1:], BF16)

    pm = pm_ref[...]
    for c in range(nblk):
        rows = pl.ds(c * PERM_ROWS, PERM_ROWS)
        for src, dst in ((q_ref, qp_ref), (k_ref, kp_ref.at[slot]), (v_ref, vp_ref.at[slot])):
            y = jnp.dot(pm, src[rows, :], preferred_element_type=F32).astype(BF16)
            for r in range(d):
                dst[r, c * chunk:(c + 1) * chunk, :] = y[r * chunk:(r + 1) * chunk, :]

    first = (n == 0).astype(jnp.int32)

    def unit(r, carry):
        out, lse = _attn_unit(qp_ref.at[r], kp_ref.at[pslot, r], kp_ref.at[slot, r],
                              vp_ref.at[pslot, r], vp_ref.at[slot, r], bias_ref, first)
        for c in range(nblk):
            dst = pl.ds(pl.multiple_of(c * PERM_ROWS + r * chunk, chunk), chunk)
            op_ref[dst, :] = out[c * chunk:(c + 1) * chunk, :]
            lp_ref[dst, :] = lse[c * chunk:(c + 1) * chunk, :]
        return carry

    lax.fori_loop(0, d, unit, 0)

    pmt = pmt_ref[...]
    for c in range(nblk):
        rows = pl.ds(c * PERM_ROWS, PERM_ROWS)
        o_ref[rows, :] = jnp.dot(pmt, op_ref[rows, :], preferred_element_type=F32).astype(BF16)
        l = lp_ref[rows, :]
        hi = l.astype(BF16)
        rem = l - hi.astype(F32)
        mid = rem.astype(BF16)
        lo = (rem - mid.astype(F32)).astype(BF16)
        lse_ref[rows, :] = (jnp.dot(pmt, hi, preferred_element_type=F32)
                            + jnp.dot(pmt, mid, preferred_element_type=F32)
                            + jnp.dot(pmt, lo, preferred_element_type=F32))


def _bucket_table(d):
    qi = jnp.arange(BLOCK, dtype=jnp.int32)[:, None]
    ki = jnp.arange(2 * BLOCK, dtype=jnp.int32)[None, :]
    dist = jnp.maximum((qi + BLOCK - ki) * d, 0)
    nf = jnp.maximum(dist, 1).astype(F32)
    large = MAX_EXACT + (jnp.log(nf / MAX_EXACT) / np.float32(np.log(MAX_DISTANCE / MAX_EXACT))
                         * (NUM_BUCKETS - MAX_EXACT)).astype(jnp.int32)
    large = jnp.minimum(large, NUM_BUCKETS - 1)
    return jnp.where(dist < MAX_EXACT, dist, large)


def _attention_group(qkv, rel_bias_g, gi, batch, seq):
    window, d = DILATION_PAIRS[gi]
    assert window // d == BLOCK
    T = batch * seq
    q_col, k_col, v_col = gi, 3 + gi, 6 + gi
    bucket = _bucket_table(d)
    out_shape = [jax.ShapeDtypeStruct((T, GROUP_WIDTH), BF16),
                 jax.ShapeDtypeStruct((T, 128), F32)]
    bias_scratch = pltpu.VMEM((2, HEADS_PER_GROUP, BLOCK, 2 * BLOCK), F32)
    smem = pl.BlockSpec(memory_space=pltpu.SMEM)
    const = lambda shape: pl.BlockSpec(shape, lambda b, n: (0,) * len(shape))

    if d == 1:
        tm = TM_ATTN_LOCAL
        nt = seq // tm
        halo_per_tile = tm // BLOCK
        tile = lambda col: pl.BlockSpec((tm, GROUP_WIDTH), lambda b, n: (b * nt + n, col))
        halo = lambda col: pl.BlockSpec(
            (BLOCK, GROUP_WIDTH),
            lambda b, n: (jnp.maximum((b * nt + n) * halo_per_tile - 1, 0), col))
        return pl.pallas_call(
            _attn_local_kernel,
            grid=(batch, nt),
            in_specs=[smem, const((BLOCK, 2 * BLOCK)),
                      tile(q_col), tile(k_col), halo(k_col), tile(v_col), halo(v_col)],
            out_specs=[pl.BlockSpec((tm, GROUP_WIDTH), lambda b, n: (b * nt + n, 0)),
                       pl.BlockSpec((tm, 128), lambda b, n: (b * nt + n, 0))],
            out_shape=out_shape,
            scratch_shapes=[bias_scratch],
            compiler_params=pltpu.CompilerParams(
                dimension_semantics=("arbitrary", "arbitrary")),
            name="attn_d1",
        )(rel_bias_g, bucket, qkv, qkv, qkv, qkv, qkv)

    tm = BLOCK * d
    assert seq % tm == 0 and tm % PERM_ROWS == 0 and PERM_ROWS % d == 0
    nt = seq // tm
    chunk = PERM_ROWS // d
    i = np.arange(PERM_ROWS)
    pm_np = np.zeros((PERM_ROWS, PERM_ROWS), np.float32)
    pm_np[i, (i % chunk) * d + i // chunk] = 1.0
    pm = jnp.asarray(pm_np, BF16)
    pmt = jnp.asarray(pm_np.T, BF16)
    tile = lambda col: pl.BlockSpec((tm, GROUP_WIDTH), lambda b, n: (b * nt + n, col))
    return pl.pallas_call(
        functools.partial(_attn_dilated_kernel, d),
        grid=(batch, nt),
        in_specs=[smem, const((BLOCK, 2 * BLOCK)),
                  const((PERM_ROWS, PERM_ROWS)), const((PERM_ROWS, PERM_ROWS)),
                  tile(q_col), tile(k_col), tile(v_col)],
        out_specs=[pl.BlockSpec((tm, GROUP_WIDTH), lambda b, n: (b * nt + n, 0)),
                   pl.BlockSpec((tm, 128), lambda b, n: (b * nt + n, 0))],
        out_shape=out_shape,
        scratch_shapes=[
            bias_scratch,
            pltpu.VMEM((d, BLOCK, GROUP_WIDTH), BF16),
            pltpu.VMEM((2, d, BLOCK, GROUP_WIDTH), BF16),
            pltpu.VMEM((2, d, BLOCK, GROUP_WIDTH), BF16),
            pltpu.VMEM((tm, GROUP_WIDTH), BF16),
            pltpu.VMEM((tm, 128), F32),
        ],
        compiler_params=pltpu.CompilerParams(
            dimension_semantics=("arbitrary", "arbitrary"),
            vmem_limit_bytes=48 * MIB),
        name=f"attn_d{d}",
    )(rel_bias_g, bucket, pm, pmt, qkv, qkv, qkv)


def _mix_out_kernel(tiles_per_seq,
                    u_ref, halo_ref, pw_ref, ps_ref,
                    o1_ref, o2_ref, o3_ref, l1_ref, l2_ref, l3_ref,
                    wo_ref, x_ref, g_ref,
                    h_ref, hn_ref, ext_ref, a_ref):
    i = pl.program_id(0)
    tm = u_ref.shape[0]
    seq_tile = i % tiles_per_seq

    halo = halo_ref[...]
    ext_ref[0:POOL_HALO, :] = jnp.where(seq_tile > 0, halo, jnp.zeros_like(halo))
    ext_ref[POOL_HALO:, :] = u_ref[...]
    pos = seq_tile * tm + lax.broadcasted_iota(jnp.int32, (tm, 1), 0)
    for gi, w in enumerate(POOL_WINDOWS):
        cs = slice(gi * POOL_GROUP_DIM, (gi + 1) * POOL_GROUP_DIM)
        cur = ext_ref[POOL_HALO:, cs]
        tot = cur
        for k in range(1, w):
            tot = tot + ext_ref[POOL_HALO - k:POOL_HALO - k + tm, cs]
        cnt = jnp.minimum(pos + 1, w).astype(F32)
        pooled = tot / cnt - cur
        mixed = jnp.dot(pooled.astype(BF16), pw_ref[gi], preferred_element_type=F32)
        a_ref[:, cs] = (mixed * ps_ref[:, cs]).astype(BF16)

    l1, l2, l3 = l1_ref[...], l2_ref[...], l3_ref[...]
    m = jnp.maximum(jnp.maximum(l1, l2), l3)
    e1, e2, e3 = jnp.exp(l1 - m), jnp.exp(l2 - m), jnp.exp(l3 - m)
    tot_e = e1 + e2 + e3
    for gi, (o_ref, e) in enumerate(((o1_ref, e1), (o2_ref, e2), (o3_ref, e3))):
        alpha = e / tot_e
        for h in range(HEADS_PER_GROUP):
            a_h = alpha[:, h * LSE_LANES:h * LSE_LANES + 1]
            hs = slice(h * HEAD_DIM, (h + 1) * HEAD_DIM)
            col = POOL_WIDTH + gi * GROUP_WIDTH + h * HEAD_DIM
            a_ref[:, col:col + HEAD_DIM] = (o_ref[:, hs].astype(F32) * a_h).astype(BF16)

    h = x_ref[...] + jnp.dot(a_ref[...], wo_ref[...], preferred_element_type=F32)
    h_ref[...] = h
    hn_ref[...] = (h * _rms_scale(h) * g_ref[...]).astype(BF16)


def _mix_out(u, pool_w, pool_scale, outs, lses, w_out, x2, g_ffn, seq):
    T = x2.shape[0]
    tm = TM_MIX
    assert seq % tm == 0 and tm % POOL_HALO == 0
    row = lambda w: pl.BlockSpec((tm, w), lambda i: (i, 0))
    const = lambda shape: pl.BlockSpec(shape, lambda i: (0,) * len(shape))
    halo_blocks = tm // POOL_HALO
    return pl.pallas_call(
        functools.partial(_mix_out_kernel, seq // tm),
        grid=(T // tm,),
        in_specs=[
            row(POOL_WIDTH),
            pl.BlockSpec((POOL_HALO, POOL_WIDTH),
                         lambda i: (jnp.maximum(i * halo_blocks - 1, 0), 0)),
            const((len(POOL_WINDOWS), POOL_GROUP_DIM, POOL_GROUP_DIM)),
            const((1, POOL_WIDTH)),
            row(GROUP_WIDTH), row(GROUP_WIDTH), row(GROUP_WIDTH),
            row(128), row(128), row(128),
            pl.BlockSpec((D_MODEL, D_MODEL), lambda i: (0, 0), pipeline_mode=pl.Buffered(1)),
            row(D_MODEL),
            const((1, D_MODEL)),
        ],
        out_specs=[row(D_MODEL), row(D_MODEL)],
        out_shape=[
            jax.ShapeDtypeStruct((T, D_MODEL), F32),
            jax.ShapeDtypeStruct((T, D_MODEL), BF16),
        ],
        scratch_shapes=[
            pltpu.VMEM((POOL_HALO + tm, POOL_WIDTH), F32),
            pltpu.VMEM((tm, D_MODEL), BF16),
        ],
        compiler_params=pltpu.CompilerParams(
            dimension_semantics=("arbitrary",),
            vmem_limit_bytes=52 * MIB),
        name="mix_out",
    )(u, u, pool_w, pool_scale, *outs, *lses, w_out, x2, g_ffn)


def _ffn_kernel(hn_ref, h_ref, wg_ref, wu_ref, wd_ref, o_ref):
    f = pl.program_id(1)

    @pl.when(f == 0)
    def _():
        o_ref[...] = h_ref[...]

    hn = hn_ref[...]
    g = jnp.dot(hn, wg_ref[...], preferred_element_type=F32)
    u = jnp.dot(hn, wu_ref[...], preferred_element_type=F32)
    a = (g * jax.nn.sigmoid(g) * u).astype(BF16)
    o_ref[...] += jnp.dot(a, wd_ref[...], preferred_element_type=F32)


def _ffn(hn, h1, w_gate, w_up, w_down):
    T = hn.shape[0]
    tm, tf = TM_FFN, TF_FFN
    return pl.pallas_call(
        _ffn_kernel,
        grid=(T // tm, D_FF // tf),
        in_specs=[
            pl.BlockSpec((tm, D_MODEL), lambda i, f: (i, 0)),
            pl.BlockSpec((tm, D_MODEL), lambda i, f: (i, 0)),
            pl.BlockSpec((D_MODEL, tf), lambda i, f: (0, f)),
            pl.BlockSpec((D_MODEL, tf), lambda i, f: (0, f)),
            pl.BlockSpec((tf, D_MODEL), lambda i, f: (f, 0)),
        ],
        out_specs=pl.BlockSpec((tm, D_MODEL), lambda i, f: (i, 0)),
        out_shape=jax.ShapeDtypeStruct((T, D_MODEL), F32),
        compiler_params=pltpu.CompilerParams(
            dimension_semantics=("arbitrary", "arbitrary"),
            vmem_limit_bytes=48 * MIB),
        name="ffn",
    )(hn, h1, w_gate, w_up, w_down)


def _ple_kernel(h_ref, p_ref, gp_ref, wpg_ref, wpp_ref, gf_ref, y_ref):
    h = h_ref[...]
    hn = (h * _rms_scale(h) * gp_ref[...]).astype(BF16)
    gate = jax.nn.sigmoid(jnp.dot(hn, wpg_ref[...], preferred_element_type=F32))
    emb = jnp.dot(p_ref[...].astype(BF16), wpp_ref[...], preferred_element_type=F32)
    h = h + gate * emb
    y_ref[...] = h * _rms_scale(h) * gf_ref[...]


def _ple_final(h2, p2, g_ple, w_pg, w_pp, g_final):
    T = h2.shape[0]
    tm = TM_PLE
    return pl.pallas_call(
        _ple_kernel,
        grid=(T // tm,),
        in_specs=[
            pl.BlockSpec((tm, D_MODEL), lambda i: (i, 0)),
            pl.BlockSpec((tm, D_PLE), lambda i: (i, 0)),
            pl.BlockSpec((1, D_MODEL), lambda i: (0, 0)),
            pl.BlockSpec((D_MODEL, D_MODEL), lambda i: (0, 0), pipeline_mode=pl.Buffered(1)),
            pl.BlockSpec((D_PLE, D_MODEL), lambda i: (0, 0)),
            pl.BlockSpec((1, D_MODEL), lambda i: (0, 0)),
        ],
        out_specs=pl.BlockSpec((tm, D_MODEL), lambda i: (i, 0)),
        out_shape=jax.ShapeDtypeStruct((T, D_MODEL), F32),
        compiler_params=pltpu.CompilerParams(
            dimension_semantics=("arbitrary",),
            vmem_limit_bytes=48 * MIB),
        name="ple_final",
    )(h2, p2, g_ple, w_pg, w_pp, g_final)


def kernel(x, p, rel_bias, norm_mix_g, w_in, pool_w, pool_scale, w_out, norm_ffn_g,
           w_gate, w_up, w_down, norm_ple_g, w_ple_gate, w_ple_proj, final_norm_g):
    B, S, D = x.shape
    depth = w_in.shape[0]
    T = B * S
    h = x.reshape(T, D)
    row = lambda v: v.reshape(1, -1).astype(F32)
    for i in range(depth):
        u, qkv = _in_proj(h, row(norm_mix_g[i]), w_in[i].astype(BF16))
        outs, lses = [], []
        for gi in range(len(DILATION_PAIRS)):
            hs = slice(gi * HEADS_PER_GROUP, (gi + 1) * HEADS_PER_GROUP)
            o, l = _attention_group(qkv, rel_bias[:, hs].astype(F32), gi, B, S)
            outs.append(o)
            lses.append(l)
        h1, hn = _mix_out(u, pool_w[i].astype(BF16), row(pool_scale[i]), outs, lses,
                          w_out[i].astype(BF16), h, row(norm_ffn_g[i]), S)
        h2 = _ffn(hn, h1, w_gate[i].astype(BF16), w_up[i].astype(BF16), w_down[i].astype(BF16))
        assert depth == 1
        h = _ple_final(h2, p[i].reshape(T, D_PLE), row(norm_ple_g[i]),
                       w_ple_gate[i].astype(BF16), w_ple_proj[i].astype(BF16), row(final_norm_g))
    return h.reshape(B, S, D)
```

```python
import functools

import jax
import jax.numpy as jnp
import numpy as np
from jax import lax
from jax.experimental import pallas as pl
from jax.experimental.pallas import tpu as pltpu

F32 = jnp.float32
BF16 = jnp.bfloat16

D_MODEL = 2048
POOL_WINDOWS = (2, 4, 8, 16)
POOL_WIDTH = 512
POOL_GROUP_DIM = 128
ATTN_WIDTH = 1536
HEAD_DIM = 128
DILATION_PAIRS = ((128, 1), (512, 4), (2048, 16))
HEADS_PER_GROUP = 4
GROUP_WIDTH = HEADS_PER_GROUP * HEAD_DIM
BLOCK = 128
D_FF = 5632
D_PLE = 256
NUM_BUCKETS = 32
MAX_EXACT = 16
MAX_DISTANCE = 2048
EPS = 1e-6
NEG_INF = -1e30
LOG2E = float(np.log2(np.e))
LN2 = float(np.log(2.0))
IN_PROJ_WIDTH = POOL_WIDTH + 3 * ATTN_WIDTH
QKV_WIDTH = 3 * ATTN_WIDTH
LSE_LANES = 128 // HEADS_PER_GROUP
POOL_HALO = 16
PERM_ROWS = 256

MIB = 1024 * 1024

TM_IN, TN_IN = 512, 1536
TM_ATTN_LOCAL = 512
TM_MIX = 512
TM_FFN, TF_FFN = 1024, 512
TM_PLE, SUB_PLE = 512, 512


def _rms_scale(x):
    return lax.rsqrt(jnp.mean(x * x, axis=-1, keepdims=True) + EPS)


def _in_proj_kernel(n_tiles, x_ref, g_ref, w_ref, u_ref, qkv_ref, xn_ref):
    s = pl.program_id(0)

    def norm():
        x = x_ref[...]
        xn_ref[s % 2] = (x * _rms_scale(x) * g_ref[...]).astype(BF16)

    def project():
        xn = xn_ref[(s - 1) % 2]
        u_ref[...] = jnp.dot(xn, w_ref[:, :POOL_WIDTH], preferred_element_type=F32)
        for c in range(QKV_WIDTH // TN_IN):
            cols = slice(POOL_WIDTH + c * TN_IN, POOL_WIDTH + (c + 1) * TN_IN)
            qkv_ref[:, c * TN_IN:(c + 1) * TN_IN] = jnp.dot(
                xn, w_ref[:, cols], preferred_element_type=F32).astype(BF16)

    @pl.when(s == 0)
    def _():
        norm()

    @pl.when((s > 0) & (s < n_tiles))
    def _():
        norm()
        project()

    @pl.when(s == n_tiles)
    def _():
        project()


def _in_proj(x2, g, w_in):
    T = x2.shape[0]
    n_tiles = T // TM_IN
    out_row = lambda s: jnp.maximum(s - 1, 0)
    return pl.pallas_call(
        functools.partial(_in_proj_kernel, n_tiles),
        grid=(n_tiles + 1,),
        in_specs=[
            pl.BlockSpec((TM_IN, D_MODEL), lambda s: (jnp.minimum(s, n_tiles - 1), 0)),
            pl.BlockSpec((1, D_MODEL), lambda s: (0, 0)),
            pl.BlockSpec((D_MODEL, IN_PROJ_WIDTH), lambda s: (0, 0), pipeline_mode=pl.Buffered(1)),
        ],
        out_specs=[
            pl.BlockSpec((TM_IN, POOL_WIDTH), lambda s: (out_row(s), 0)),
            pl.BlockSpec((TM_IN, QKV_WIDTH), lambda s: (out_row(s), 0)),
        ],
        out_shape=[
            jax.ShapeDtypeStruct((T, POOL_WIDTH), F32),
            jax.ShapeDtypeStruct((T, QKV_WIDTH), BF16),
        ],
        scratch_shapes=[pltpu.VMEM((2, TM_IN, D_MODEL), BF16)],
        compiler_params=pltpu.CompilerParams(
            dimension_semantics=("arbitrary",),
            vmem_limit_bytes=56 * MIB),
        name="in_proj",
    )(x2, g, w_in)


def _build_bias(rb_ref, idx_ref, bias_ref):
    idx = idx_ref[...]
    qi = lax.broadcasted_iota(jnp.int32, (BLOCK, 2 * BLOCK), 0)
    ki = lax.broadcasted_iota(jnp.int32, (BLOCK, 2 * BLOCK), 1)
    off = qi + BLOCK - ki
    band = (off >= 0) & (off <= BLOCK)
    band_first = band & (ki >= BLOCK)
    mult = np.float32(LOG2E * HEAD_DIM ** -0.5)
    bias_ref[0, HEADS_PER_GROUP] = jnp.where(band, mult, 0.0)
    bias_ref[1, HEADS_PER_GROUP] = jnp.where(band_first, mult, 0.0)
    for h in range(HEADS_PER_GROUP):
        acc = jnp.zeros((BLOCK, 2 * BLOCK), F32)
        for bkt in range(NUM_BUCKETS):
            acc = jnp.where(idx == bkt, rb_ref[bkt, h], acc)
        bias_ref[0, h] = jnp.where(band, acc, NEG_INF) * np.float32(LOG2E)
        bias_ref[1, h] = jnp.where(band_first, acc, NEG_INF) * np.float32(LOG2E)


def _attn_unit(q_ref, kp_ref, kc_ref, vp_ref, vc_ref, bias_ref, first):
    lane = lax.broadcasted_iota(jnp.int32, (1, 128), 1)
    mult = bias_ref[first, HEADS_PER_GROUP]
    outs = []
    lse_packed = jnp.zeros((BLOCK, 128), F32)
    for h in range(HEADS_PER_GROUP):
        hs = slice(h * HEAD_DIM, (h + 1) * HEAD_DIM)
        k = jnp.concatenate([kp_ref[:, hs], kc_ref[:, hs]], axis=0)
        v = jnp.concatenate([vp_ref[:, hs], vc_ref[:, hs]], axis=0)
        s = lax.dot_general(q_ref[:, hs], k, (((1,), (1,)), ((), ())),
                            preferred_element_type=F32)
        logits2 = s * mult + bias_ref[first, h]
        m2 = jnp.max(logits2, axis=-1, keepdims=True)
        p = jnp.exp2(logits2 - m2)
        den = jnp.sum(p, axis=-1, keepdims=True)
        pv = jnp.dot(p.astype(BF16), v, preferred_element_type=F32)
        outs.append((pv / den).astype(BF16))
        lse = m2 * np.float32(LN2) + jnp.log(den)
        lse_packed = jnp.where((lane >= h * LSE_LANES) & (lane < (h + 1) * LSE_LANES),
                               lse, lse_packed)
    return jnp.concatenate(outs, axis=1), lse_packed


def _attn_local_kernel(rb_ref, idx_ref, q_ref, k_ref, kh_ref, v_ref, vh_ref,
                       o_ref, lse_ref, bias_ref):
    b, n = pl.program_id(0), pl.program_id(1)

    @pl.when((b == 0) & (n == 0))
    def _():
        _build_bias(rb_ref, idx_ref, bias_ref)

    first = (n == 0).astype(jnp.int32)
    for i in range(q_ref.shape[0] // BLOCK):
        rows = pl.ds(i * BLOCK, BLOCK)
        if i == 0:
            kp, vp, f = kh_ref, vh_ref, first
        else:
            prev_rows = pl.ds((i - 1) * BLOCK, BLOCK)
            kp, vp, f = k_ref.at[prev_rows], v_ref.at[prev_rows], 0
        out, lse = _attn_unit(q_ref.at[rows], kp, k_ref.at[rows], vp, v_ref.at[rows], bias_ref, f)
        o_ref[rows, :] = out
        lse_ref[rows, :] = lse


def _attn_dilated_kernel(d, rb_ref, idx_ref, pm_ref, pmt_ref, q_ref, k_ref, v_ref,
                         o_ref, lse_ref, bias_ref, qp_ref, kp_ref, vp_ref, op_ref, lp_ref):
    b, n = pl.program_id(0), pl.program_id(1)
    nblk = q_ref.shape[0] // PERM_ROWS
    chunk = PERM_ROWS // d
    slot = n % 2
    pslot = 1 - slot

    @pl.when((b == 0) & (n == 0))
    def _():
        _build_bias(rb_ref, idx_ref, bias_ref)

    @pl.when(n == 0)
    def _():
        kp_ref[pslot] = jnp.zeros(kp_ref.shape[1:], BF16)
        vp_ref[pslot] = jnp.zeros(vp_ref.shape[1:], BF16)

    pm = pm_ref[...]
    for c in range(nblk):
        rows = pl.ds(c * PERM_ROWS, PERM_ROWS)
        for src, dst in ((q_ref, qp_ref), (k_ref, kp_ref.at[slot]), (v_ref, vp_ref.at[slot])):
            y = jnp.dot(pm, src[rows, :], preferred_element_type=F32).astype(BF16)
            for r in range(d):
                dst[r, c * chunk:(c + 1) * chunk, :] = y[r * chunk:(r + 1) * chunk, :]

    first = (n == 0).astype(jnp.int32)

    def unit(r, carry):
        out, lse = _attn_unit(qp_ref.at[r], kp_ref.at[pslot, r], kp_ref.at[slot, r],
                              vp_ref.at[pslot, r], vp_ref.at[slot, r], bias_ref, first)
        for c in range(nblk):
            dst = pl.ds(pl.multiple_of(c * PERM_ROWS + r * chunk, chunk), chunk)
            op_ref[dst, :] = out[c * chunk:(c + 1) * chunk, :]
            lp_ref[dst, :] = lse[c * chunk:(c + 1) * chunk, :]
        return carry

    lax.fori_loop(0, d, unit, 0, unroll=4)

    pmt = pmt_ref[...]
    for c in range(nblk):
        rows = pl.ds(c * PERM_ROWS, PERM_ROWS)
        o_ref[rows, :] = jnp.dot(pmt, op_ref[rows, :], preferred_element_type=F32).astype(BF16)
        l = lp_ref[rows, :]
        hi = l.astype(BF16)
        rem = l - hi.astype(F32)
        mid = rem.astype(BF16)
        lo = (rem - mid.astype(F32)).astype(BF16)
        lse_ref[rows, :] = (jnp.dot(pmt, hi, preferred_element_type=F32)
                            + jnp.dot(pmt, mid, preferred_element_type=F32)
                            + jnp.dot(pmt, lo, preferred_element_type=F32))


def _bucket_table(d):
    qi = jnp.arange(BLOCK, dtype=jnp.int32)[:, None]
    ki = jnp.arange(2 * BLOCK, dtype=jnp.int32)[None, :]
    dist = jnp.maximum((qi + BLOCK - ki) * d, 0)
    nf = jnp.maximum(dist, 1).astype(F32)
    large = MAX_EXACT + (jnp.log(nf / MAX_EXACT) / np.float32(np.log(MAX_DISTANCE / MAX_EXACT))
                         * (NUM_BUCKETS - MAX_EXACT)).astype(jnp.int32)
    large = jnp.minimum(large, NUM_BUCKETS - 1)
    return jnp.where(dist < MAX_EXACT, dist, large)


def _attention_group(qkv, rel_bias_g, gi, batch, seq):
    window, d = DILATION_PAIRS[gi]
    assert window // d == BLOCK
    T = batch * seq
    q_col, k_col, v_col = gi, 3 + gi, 6 + gi
    bucket = _bucket_table(d)
    out_shape = [jax.ShapeDtypeStruct((T, GROUP_WIDTH), BF16),
                 jax.ShapeDtypeStruct((T, 128), F32)]
    bias_scratch = pltpu.VMEM((2, HEADS_PER_GROUP + 1, BLOCK, 2 * BLOCK), F32)
    smem = pl.BlockSpec(memory_space=pltpu.SMEM)
    const = lambda shape: pl.BlockSpec(shape, lambda b, n: (0,) * len(shape))

    if d == 1:
        tm = TM_ATTN_LOCAL
        nt = seq // tm
        halo_per_tile = tm // BLOCK
        tile = lambda col: pl.BlockSpec((tm, GROUP_WIDTH), lambda b, n: (b * nt + n, col))
        halo = lambda col: pl.BlockSpec(
            (BLOCK, GROUP_WIDTH),
            lambda b, n: (jnp.maximum((b * nt + n) * halo_per_tile - 1, 0), col))
        return pl.pallas_call(
            _attn_local_kernel,
            grid=(batch, nt),
            in_specs=[smem, const((BLOCK, 2 * BLOCK)),
                      tile(q_col), tile(k_col), halo(k_col), tile(v_col), halo(v_col)],
            out_specs=[pl.BlockSpec((tm, GROUP_WIDTH), lambda b, n: (b * nt + n, 0)),
                       pl.BlockSpec((tm, 128), lambda b, n: (b * nt + n, 0))],
            out_shape=out_shape,
            scratch_shapes=[bias_scratch],
            compiler_params=pltpu.CompilerParams(
                dimension_semantics=("arbitrary", "arbitrary")),
            name="attn_d1",
        )(rel_bias_g, bucket, qkv, qkv, qkv, qkv, qkv)

    tm = BLOCK * d
    assert seq % tm == 0 and tm % PERM_ROWS == 0 and PERM_ROWS % d == 0
    nt = seq // tm
    chunk = PERM_ROWS // d
    i = np.arange(PERM_ROWS)
    pm_np = np.zeros((PERM_ROWS, PERM_ROWS), np.float32)
    pm_np[i, (i % chunk) * d + i // chunk] = 1.0
    pm = jnp.asarray(pm_np, BF16)
    pmt = jnp.asarray(pm_np.T, BF16)
    tile = lambda col: pl.BlockSpec((tm, GROUP_WIDTH), lambda b, n: (b * nt + n, col))
    return pl.pallas_call(
        functools.partial(_attn_dilated_kernel, d),
        grid=(batch, nt),
        in_specs=[smem, const((BLOCK, 2 * BLOCK)),
                  const((PERM_ROWS, PERM_ROWS)), const((PERM_ROWS, PERM_ROWS)),
                  tile(q_col), tile(k_col), tile(v_col)],
        out_specs=[pl.BlockSpec((tm, GROUP_WIDTH), lambda b, n: (b * nt + n, 0)),
                   pl.BlockSpec((tm, 128), lambda b, n: (b * nt + n, 0))],
        out_shape=out_shape,
        scratch_shapes=[
            bias_scratch,
            pltpu.VMEM((d, BLOCK, GROUP_WIDTH), BF16),
            pltpu.VMEM((2, d, BLOCK, GROUP_WIDTH), BF16),
            pltpu.VMEM((2, d, BLOCK, GROUP_WIDTH), BF16),
            pltpu.VMEM((tm, GROUP_WIDTH), BF16),
            pltpu.VMEM((tm, 128), F32),
        ],
        compiler_params=pltpu.CompilerParams(
            dimension_semantics=("arbitrary", "arbitrary"),
            vmem_limit_bytes=48 * MIB),
        name=f"attn_d{d}",
    )(rel_bias_g, bucket, pm, pmt, qkv, qkv, qkv)


def _mix_out_kernel(tiles_per_seq,
                    u_ref, halo_ref, pw_ref, ps_ref,
                    o1_ref, o2_ref, o3_ref, l1_ref, l2_ref, l3_ref,
                    wo_ref, x_ref, g_ref,
                    h_ref, hn_ref, ext_ref, a_ref):
    i = pl.program_id(0)
    tm = u_ref.shape[0]
    seq_tile = i % tiles_per_seq

    halo = halo_ref[...]
    ext_ref[0:POOL_HALO, :] = jnp.where(seq_tile > 0, halo, jnp.zeros_like(halo))
    ext_ref[POOL_HALO:, :] = u_ref[...]
    pos = seq_tile * tm + lax.broadcasted_iota(jnp.int32, (tm, 1), 0)
    for gi, w in enumerate(POOL_WINDOWS):
        cs = slice(gi * POOL_GROUP_DIM, (gi + 1) * POOL_GROUP_DIM)
        cur = ext_ref[POOL_HALO:, cs]
        tot = cur
        for k in range(1, w):
            tot = tot + ext_ref[POOL_HALO - k:POOL_HALO - k + tm, cs]
        cnt = jnp.minimum(pos + 1, w).astype(F32)
        pooled = tot / cnt - cur
        mixed = jnp.dot(pooled.astype(BF16), pw_ref[gi], preferred_element_type=F32)
        a_ref[:, cs] = (mixed * ps_ref[:, cs]).astype(BF16)

    l1, l2, l3 = l1_ref[...], l2_ref[...], l3_ref[...]
    m = jnp.maximum(jnp.maximum(l1, l2), l3)
    e1, e2, e3 = jnp.exp(l1 - m), jnp.exp(l2 - m), jnp.exp(l3 - m)
    tot_e = e1 + e2 + e3
    for gi, (o_ref, e) in enumerate(((o1_ref, e1), (o2_ref, e2), (o3_ref, e3))):
        alpha = e / tot_e
        for h in range(HEADS_PER_GROUP):
            a_h = alpha[:, h * LSE_LANES:h * LSE_LANES + 1]
            hs = slice(h * HEAD_DIM, (h + 1) * HEAD_DIM)
            col = POOL_WIDTH + gi * GROUP_WIDTH + h * HEAD_DIM
            a_ref[:, col:col + HEAD_DIM] = (o_ref[:, hs].astype(F32) * a_h).astype(BF16)

    h = x_ref[...] + jnp.dot(a_ref[...], wo_ref[...], preferred_element_type=F32)
    h_ref[...] = h
    hn_ref[...] = (h * _rms_scale(h) * g_ref[...]).astype(BF16)


def _mix_out(u, pool_w, pool_scale, outs, lses, w_out, x2, g_ffn, seq):
    T = x2.shape[0]
    tm = TM_MIX
    assert seq % tm == 0 and tm % POOL_HALO == 0
    row = lambda w: pl.BlockSpec((tm, w), lambda i: (i, 0))
    const = lambda shape: pl.BlockSpec(shape, lambda i: (0,) * len(shape))
    halo_blocks = tm // POOL_HALO
    return pl.pallas_call(
        functools.partial(_mix_out_kernel, seq // tm),
        grid=(T // tm,),
        in_specs=[
            row(POOL_WIDTH),
            pl.BlockSpec((POOL_HALO, POOL_WIDTH),
                         lambda i: (jnp.maximum(i * halo_blocks - 1, 0), 0)),
            const((len(POOL_WINDOWS), POOL_GROUP_DIM, POOL_GROUP_DIM)),
            const((1, POOL_WIDTH)),
            row(GROUP_WIDTH), row(GROUP_WIDTH), row(GROUP_WIDTH),
            row(128), row(128), row(128),
            pl.BlockSpec((D_MODEL, D_MODEL), lambda i: (0, 0), pipeline_mode=pl.Buffered(1)),
            row(D_MODEL),
            const((1, D_MODEL)),
        ],
        out_specs=[row(D_MODEL), row(D_MODEL)],
        out_shape=[
            jax.ShapeDtypeStruct((T, D_MODEL), F32),
            jax.ShapeDtypeStruct((T, D_MODEL), BF16),
        ],
        scratch_shapes=[
            pltpu.VMEM((POOL_HALO + tm, POOL_WIDTH), F32),
            pltpu.VMEM((tm, D_MODEL), BF16),
        ],
        compiler_params=pltpu.CompilerParams(
            dimension_semantics=("arbitrary",),
            vmem_limit_bytes=52 * MIB),
        name="mix_out",
    )(u, u, pool_w, pool_scale, *outs, *lses, w_out, x2, g_ffn)


def _ffn_kernel(nf, last, hn_ref, wg_ref, wu_ref, wd_ref, o_ref, a_ref):
    s = pl.program_id(0)
    chunk_d = lax.rem(jnp.maximum(s - 1, 0), nf)
    mid = (s > 0) & (s < last)

    def gate_up():
        hn = hn_ref[...]
        g = jnp.dot(hn, wg_ref[...], preferred_element_type=F32)
        u = jnp.dot(hn, wu_ref[...], preferred_element_type=F32)
        a_ref[s % 2] = (g * jax.nn.sigmoid(g) * u).astype(BF16)

    def down():
        return jnp.dot(a_ref[(s - 1) % 2], wd_ref[...], preferred_element_type=F32)

    @pl.when(s == 0)
    def _():
        gate_up()

    @pl.when(mid & (chunk_d == 0))
    def _():
        gate_up()
        o_ref[...] = down()

    @pl.when(mid & (chunk_d > 0))
    def _():
        gate_up()
        o_ref[...] += down()

    @pl.when(s == last)
    def _():
        o_ref[...] += down()


def _ffn(hn, w_gate, w_up, w_down):
    T = hn.shape[0]
    tm, tf = TM_FFN, TF_FFN
    nf = D_FF // tf
    last = (T // tm) * nf
    up_step = lambda s: jnp.minimum(s, last - 1)
    down_step = lambda s: jnp.maximum(s - 1, 0)
    return pl.pallas_call(
        functools.partial(_ffn_kernel, nf, last),
        grid=(last + 1,),
        in_specs=[
            pl.BlockSpec((tm, D_MODEL), lambda s: (lax.div(up_step(s), nf), 0)),
            pl.BlockSpec((D_MODEL, tf), lambda s: (0, lax.rem(up_step(s), nf))),
            pl.BlockSpec((D_MODEL, tf), lambda s: (0, lax.rem(up_step(s), nf))),
            pl.BlockSpec((tf, D_MODEL), lambda s: (lax.rem(down_step(s), nf), 0)),
        ],
        out_specs=pl.BlockSpec((tm, D_MODEL), lambda s: (lax.div(down_step(s), nf), 0)),
        out_shape=jax.ShapeDtypeStruct((T, D_MODEL), F32),
        scratch_shapes=[pltpu.VMEM((2, tm, tf), BF16)],
        compiler_params=pltpu.CompilerParams(
            dimension_semantics=("arbitrary",),
            vmem_limit_bytes=56 * MIB),
        name="ffn",
    )(hn, w_gate, w_up, w_down)


def _ple_kernel(h_ref, f_ref, p_ref, gp_ref, wpg_ref, wpp_ref, gf_ref, y_ref):
    for c in range(h_ref.shape[0] // SUB_PLE):
        rows = pl.ds(c * SUB_PLE, SUB_PLE)
        h = h_ref[rows, :] + f_ref[rows, :]
        hn = (h * _rms_scale(h) * gp_ref[...]).astype(BF16)
        gate = jax.nn.sigmoid(jnp.dot(hn, wpg_ref[...], preferred_element_type=F32))
        emb = jnp.dot(p_ref[rows, :].astype(BF16), wpp_ref[...], preferred_element_type=F32)
        h = h + gate * emb
        y_ref[rows, :] = h * _rms_scale(h) * gf_ref[...]


def _ple_final(h1, ffn, p2, g_ple, w_pg, w_pp, g_final):
    T = h1.shape[0]
    tm = TM_PLE
    return pl.pallas_call(
        _ple_kernel,
        grid=(T // tm,),
        in_specs=[
            pl.BlockSpec((tm, D_MODEL), lambda i: (i, 0)),
            pl.BlockSpec((tm, D_MODEL), lambda i: (i, 0)),
            pl.BlockSpec((tm, D_PLE), lambda i: (i, 0)),
            pl.BlockSpec((1, D_MODEL), lambda i: (0, 0)),
            pl.BlockSpec((D_MODEL, D_MODEL), lambda i: (0, 0), pipeline_mode=pl.Buffered(1)),
            pl.BlockSpec((D_PLE, D_MODEL), lambda i: (0, 0)),
            pl.BlockSpec((1, D_MODEL), lambda i: (0, 0)),
        ],
        out_specs=pl.BlockSpec((tm, D_MODEL), lambda i: (i, 0)),
        out_shape=jax.ShapeDtypeStruct((T, D_MODEL), F32),
        compiler_params=pltpu.CompilerParams(
            dimension_semantics=("arbitrary",),
            vmem_limit_bytes=48 * MIB),
        name="ple_final",
    )(h1, ffn, p2, g_ple, w_pg, w_pp, g_final)


def kernel(x, p, rel_bias, norm_mix_g, w_in, pool_w, pool_scale, w_out, norm_ffn_g,
           w_gate, w_up, w_down, norm_ple_g, w_ple_gate, w_ple_proj, final_norm_g):
    B, S, D = x.shape
    depth = w_in.shape[0]
    T = B * S
    h = x.reshape(T, D)
    row = lambda v: v.reshape(1, -1).astype(F32)
    for i in range(depth):
        u, qkv = _in_proj(h, row(norm_mix_g[i]), w_in[i].astype(BF16))
        outs, lses = [], []
        for gi in range(len(DILATION_PAIRS)):
            hs = slice(gi * HEADS_PER_GROUP, (gi + 1) * HEADS_PER_GROUP)
            o, l = _attention_group(qkv, rel_bias[:, hs].astype(F32), gi, B, S)
            outs.append(o)
            lses.append(l)
        h1, hn = _mix_out(u, pool_w[i].astype(BF16), row(pool_scale[i]), outs, lses,
                          w_out[i].astype(BF16), h, row(norm_ffn_g[i]), S)
        ffn = _ffn(hn, w_gate[i].astype(BF16), w_up[i].astype(BF16), w_down[i].astype(BF16))
        assert depth == 1
        h = _ple_final(h1, ffn, p[i].reshape(T, D_PLE), row(norm_ple_g[i]),
                       w_ple_gate[i].astype(BF16), w_ple_proj[i].astype(BF16), row(final_norm_g))
    return h.reshape(B, S, D)
```

```python
import functools

import jax
import jax.numpy as jnp
import numpy as np
from jax import lax
from jax.experimental import pallas as pl
from jax.experimental.pallas import tpu as pltpu

F32 = jnp.float32
BF16 = jnp.bfloat16

D_MODEL = 2048
POOL_WINDOWS = (2, 4, 8, 16)
POOL_WIDTH = 512
POOL_GROUP_DIM = 128
ATTN_WIDTH = 1536
HEAD_DIM = 128
DILATION_PAIRS = ((128, 1), (512, 4), (2048, 16))
HEADS_PER_GROUP = 4
GROUP_WIDTH = HEADS_PER_GROUP * HEAD_DIM
BLOCK = 128
D_FF = 5632
D_PLE = 256
NUM_BUCKETS = 32
MAX_EXACT = 16
MAX_DISTANCE = 2048
EPS = 1e-6
NEG_INF = -1e30
LOG2E = float(np.log2(np.e))
LN2 = float(np.log(2.0))
IN_PROJ_WIDTH = POOL_WIDTH + 3 * ATTN_WIDTH
QKV_WIDTH = 3 * ATTN_WIDTH
LSE_LANES = 128 // HEADS_PER_GROUP
POOL_HALO = 16
PERM_ROWS = 256

MIB = 1024 * 1024

TM_IN, TN_IN = 512, 1536
TM_ATTN_LOCAL = 512
TM_MIX = 512
TM_FFN, TF_FFN = 1024, 512
TM_PLE, SUB_PLE = 512, 512


def _rms_scale(x):
    return lax.rsqrt(jnp.mean(x * x, axis=-1, keepdims=True) + EPS)


def _in_proj_kernel(n_tiles, x_ref, g_ref, w_ref, u_ref, qkv_ref, xn_ref):
    s = pl.program_id(0)

    def norm():
        x = x_ref[...]
        xn_ref[s % 2] = (x * _rms_scale(x) * g_ref[...]).astype(BF16)

    def project():
        xn = xn_ref[(s - 1) % 2]
        u_ref[...] = jnp.dot(xn, w_ref[:, :POOL_WIDTH], preferred_element_type=F32)
        for c in range(QKV_WIDTH // TN_IN):
            cols = slice(POOL_WIDTH + c * TN_IN, POOL_WIDTH + (c + 1) * TN_IN)
            qkv_ref[:, c * TN_IN:(c + 1) * TN_IN] = jnp.dot(
                xn, w_ref[:, cols], preferred_element_type=F32).astype(BF16)

    @pl.when(s == 0)
    def _():
        norm()

    @pl.when((s > 0) & (s < n_tiles))
    def _():
        norm()
        project()

    @pl.when(s == n_tiles)
    def _():
        project()


def _in_proj(x2, g, w_in):
    T = x2.shape[0]
    n_tiles = T // TM_IN
    out_row = lambda s: jnp.maximum(s - 1, 0)
    return pl.pallas_call(
        functools.partial(_in_proj_kernel, n_tiles),
        grid=(n_tiles + 1,),
        in_specs=[
            pl.BlockSpec((TM_IN, D_MODEL), lambda s: (jnp.minimum(s, n_tiles - 1), 0)),
            pl.BlockSpec((1, D_MODEL), lambda s: (0, 0)),
            pl.BlockSpec((D_MODEL, IN_PROJ_WIDTH), lambda s: (0, 0), pipeline_mode=pl.Buffered(1)),
        ],
        out_specs=[
            pl.BlockSpec((TM_IN, POOL_WIDTH), lambda s: (out_row(s), 0)),
            pl.BlockSpec((TM_IN, QKV_WIDTH), lambda s: (out_row(s), 0)),
        ],
        out_shape=[
            jax.ShapeDtypeStruct((T, POOL_WIDTH), F32),
            jax.ShapeDtypeStruct((T, QKV_WIDTH), BF16),
        ],
        scratch_shapes=[pltpu.VMEM((2, TM_IN, D_MODEL), BF16)],
        compiler_params=pltpu.CompilerParams(
            dimension_semantics=("arbitrary",),
            vmem_limit_bytes=56 * MIB),
        name="in_proj",
    )(x2, g, w_in)


def _build_bias(rb_ref, idx_ref, bias_ref):
    idx = idx_ref[...]
    qi = lax.broadcasted_iota(jnp.int32, (BLOCK, 2 * BLOCK), 0)
    ki = lax.broadcasted_iota(jnp.int32, (BLOCK, 2 * BLOCK), 1)
    off = qi + BLOCK - ki
    band = (off >= 0) & (off <= BLOCK)
    band_first = band & (ki >= BLOCK)
    mult = np.float32(LOG2E * HEAD_DIM ** -0.5)
    bias_ref[0, HEADS_PER_GROUP] = jnp.where(band, mult, 0.0)
    bias_ref[1, HEADS_PER_GROUP] = jnp.where(band_first, mult, 0.0)
    for h in range(HEADS_PER_GROUP):
        acc = jnp.zeros((BLOCK, 2 * BLOCK), F32)
        for bkt in range(NUM_BUCKETS):
            acc = jnp.where(idx == bkt, rb_ref[bkt, h], acc)
        bias_ref[0, h] = jnp.where(band, acc, NEG_INF) * np.float32(LOG2E)
        bias_ref[1, h] = jnp.where(band_first, acc, NEG_INF) * np.float32(LOG2E)


def _attn_unit(q_ref, kp_ref, kc_ref, vp_ref, vc_ref, bias_ref, first):
    lane = lax.broadcasted_iota(jnp.int32, (1, 128), 1)
    mult = bias_ref[first, HEADS_PER_GROUP]
    outs = []
    lse_packed = jnp.zeros((BLOCK, 128), F32)
    for h in range(HEADS_PER_GROUP):
        hs = slice(h * HEAD_DIM, (h + 1) * HEAD_DIM)
        k = jnp.concatenate([kp_ref[:, hs], kc_ref[:, hs]], axis=0)
        v = jnp.concatenate([vp_ref[:, hs], vc_ref[:, hs]], axis=0)
        s = lax.dot_general(q_ref[:, hs], k, (((1,), (1,)), ((), ())),
                            preferred_element_type=F32)
        logits2 = s * mult + bias_ref[first, h]
        m2 = jnp.max(logits2, axis=-1, keepdims=True)
        p = jnp.exp2(logits2 - m2)
        den = jnp.sum(p, axis=-1, keepdims=True)
        pv = jnp.dot(p.astype(BF16), v, preferred_element_type=F32)
        outs.append((pv / den).astype(BF16))
        lse = m2 * np.float32(LN2) + jnp.log(den)
        lse_packed = jnp.where((lane >= h * LSE_LANES) & (lane < (h + 1) * LSE_LANES),
                               lse, lse_packed)
    return jnp.concatenate(outs, axis=1), lse_packed


def _cast_side_job(n_cast, refs, n_out):
    cast_in, rest = refs[:n_cast], refs[n_cast:]
    outs, rest = rest[:n_out], rest[n_out:]
    cast_out, scratch = rest[:n_cast], rest[n_cast:]
    for src, dst in zip(cast_in, cast_out):
        dst[...] = src[...].astype(BF16)
    return outs, scratch


def _attn_local_kernel(n_cast, rb_ref, idx_ref, q_ref, k_ref, kh_ref, v_ref, vh_ref, *refs):
    (o_ref, lse_ref), (bias_ref,) = _cast_side_job(n_cast, refs, 2)
    b, n = pl.program_id(0), pl.program_id(1)

    @pl.when((b == 0) & (n == 0))
    def _():
        _build_bias(rb_ref, idx_ref, bias_ref)

    first = (n == 0).astype(jnp.int32)
    for i in range(q_ref.shape[0] // BLOCK):
        rows = pl.ds(i * BLOCK, BLOCK)
        if i == 0:
            kp, vp, f = kh_ref, vh_ref, first
        else:
            prev_rows = pl.ds((i - 1) * BLOCK, BLOCK)
            kp, vp, f = k_ref.at[prev_rows], v_ref.at[prev_rows], 0
        out, lse = _attn_unit(q_ref.at[rows], kp, k_ref.at[rows], vp, v_ref.at[rows], bias_ref, f)
        o_ref[rows, :] = out
        lse_ref[rows, :] = lse


def _attn_dilated_kernel(d, n_cast, rb_ref, idx_ref, pm_ref, pmt_ref, q_ref, k_ref, v_ref, *refs):
    (o_ref, lse_ref), scratch = _cast_side_job(n_cast, refs, 2)
    bias_ref, qp_ref, kp_ref, vp_ref, op_ref, lp_ref = scratch
    b, n = pl.program_id(0), pl.program_id(1)
    nblk = q_ref.shape[0] // PERM_ROWS
    chunk = PERM_ROWS // d
    slot = n % 2
    pslot = 1 - slot

    @pl.when((b == 0) & (n == 0))
    def _():
        _build_bias(rb_ref, idx_ref, bias_ref)

    @pl.when(n == 0)
    def _():
        kp_ref[pslot] = jnp.zeros(kp_ref.shape[1:], BF16)
        vp_ref[pslot] = jnp.zeros(vp_ref.shape[1:], BF16)

    pm = pm_ref[...]
    for c in range(nblk):
        rows = pl.ds(c * PERM_ROWS, PERM_ROWS)
        for src, dst in ((q_ref, qp_ref), (k_ref, kp_ref.at[slot]), (v_ref, vp_ref.at[slot])):
            y = jnp.dot(pm, src[rows, :], preferred_element_type=F32).astype(BF16)
            for r in range(d):
                dst[r, c * chunk:(c + 1) * chunk, :] = y[r * chunk:(r + 1) * chunk, :]

    first = (n == 0).astype(jnp.int32)

    def unit(r, carry):
        out, lse = _attn_unit(qp_ref.at[r], kp_ref.at[pslot, r], kp_ref.at[slot, r],
                              vp_ref.at[pslot, r], vp_ref.at[slot, r], bias_ref, first)
        for c in range(nblk):
            dst = pl.ds(pl.multiple_of(c * PERM_ROWS + r * chunk, chunk), chunk)
            op_ref[dst, :] = out[c * chunk:(c + 1) * chunk, :]
            lp_ref[dst, :] = lse[c * chunk:(c + 1) * chunk, :]
        return carry

    lax.fori_loop(0, d, unit, 0, unroll=4)

    pmt = pmt_ref[...]
    for c in range(nblk):
        rows = pl.ds(c * PERM_ROWS, PERM_ROWS)
        o_ref[rows, :] = jnp.dot(pmt, op_ref[rows, :], preferred_element_type=F32).astype(BF16)
        l = lp_ref[rows, :]
        hi = l.astype(BF16)
        rem = l - hi.astype(F32)
        mid = rem.astype(BF16)
        lo = (rem - mid.astype(F32)).astype(BF16)
        lse_ref[rows, :] = (jnp.dot(pmt, hi, preferred_element_type=F32)
                            + jnp.dot(pmt, mid, preferred_element_type=F32)
                            + jnp.dot(pmt, lo, preferred_element_type=F32))


def _bucket_table(d):
    qi = jnp.arange(BLOCK, dtype=jnp.int32)[:, None]
    ki = jnp.arange(2 * BLOCK, dtype=jnp.int32)[None, :]
    dist = jnp.maximum((qi + BLOCK - ki) * d, 0)
    nf = jnp.maximum(dist, 1).astype(F32)
    large = MAX_EXACT + (jnp.log(nf / MAX_EXACT) / np.float32(np.log(MAX_DISTANCE / MAX_EXACT))
                         * (NUM_BUCKETS - MAX_EXACT)).astype(jnp.int32)
    large = jnp.minimum(large, NUM_BUCKETS - 1)
    return jnp.where(dist < MAX_EXACT, dist, large)


def _attention_group(qkv, rel_bias_g, gi, batch, seq, casts=()):
    window, d = DILATION_PAIRS[gi]
    assert window // d == BLOCK
    T = batch * seq
    q_col, k_col, v_col = gi, 3 + gi, 6 + gi
    bucket = _bucket_table(d)
    tm = TM_ATTN_LOCAL if d == 1 else BLOCK * d
    assert seq % tm == 0
    nt = seq // tm
    n_steps = batch * nt
    cast_specs = []
    for w in casts:
        assert w.shape[0] % (n_steps * 16) == 0
        cast_specs.append(pl.BlockSpec((w.shape[0] // n_steps, w.shape[1]),
                                       lambda b, n: (b * nt + n, 0)))
    out_shape = [jax.ShapeDtypeStruct((T, GROUP_WIDTH), BF16),
                 jax.ShapeDtypeStruct((T, 128), F32)]
    out_shape += [jax.ShapeDtypeStruct(w.shape, BF16) for w in casts]
    out_specs = [pl.BlockSpec((tm, GROUP_WIDTH), lambda b, n: (b * nt + n, 0)),
                 pl.BlockSpec((tm, 128), lambda b, n: (b * nt + n, 0))] + cast_specs
    bias_scratch = pltpu.VMEM((2, HEADS_PER_GROUP + 1, BLOCK, 2 * BLOCK), F32)
    smem = pl.BlockSpec(memory_space=pltpu.SMEM)
    const = lambda shape: pl.BlockSpec(shape, lambda b, n: (0,) * len(shape))
    tile = lambda col: pl.BlockSpec((tm, GROUP_WIDTH), lambda b, n: (b * nt + n, col))

    if d == 1:
        halo_per_tile = tm // BLOCK
        halo = lambda col: pl.BlockSpec(
            (BLOCK, GROUP_WIDTH),
            lambda b, n: (jnp.maximum((b * nt + n) * halo_per_tile - 1, 0), col))
        return pl.pallas_call(
            functools.partial(_attn_local_kernel, len(casts)),
            grid=(batch, nt),
            in_specs=[smem, const((BLOCK, 2 * BLOCK)),
                      tile(q_col), tile(k_col), halo(k_col), tile(v_col), halo(v_col)] + cast_specs,
            out_specs=out_specs,
            out_shape=out_shape,
            scratch_shapes=[bias_scratch],
            compiler_params=pltpu.CompilerParams(
                dimension_semantics=("arbitrary", "arbitrary"),
                vmem_limit_bytes=32 * MIB),
            name="attn_d1",
        )(rel_bias_g, bucket, qkv, qkv, qkv, qkv, qkv, *casts)

    assert tm % PERM_ROWS == 0 and PERM_ROWS % d == 0
    chunk = PERM_ROWS // d
    i = np.arange(PERM_ROWS)
    pm_np = np.zeros((PERM_ROWS, PERM_ROWS), np.float32)
    pm_np[i, (i % chunk) * d + i // chunk] = 1.0
    pm = jnp.asarray(pm_np, BF16)
    pmt = jnp.asarray(pm_np.T, BF16)
    return pl.pallas_call(
        functools.partial(_attn_dilated_kernel, d, len(casts)),
        grid=(batch, nt),
        in_specs=[smem, const((BLOCK, 2 * BLOCK)),
                  const((PERM_ROWS, PERM_ROWS)), const((PERM_ROWS, PERM_ROWS)),
                  tile(q_col), tile(k_col), tile(v_col)] + cast_specs,
        out_specs=out_specs,
        out_shape=out_shape,
        scratch_shapes=[
            bias_scratch,
            pltpu.VMEM((d, BLOCK, GROUP_WIDTH), BF16),
            pltpu.VMEM((2, d, BLOCK, GROUP_WIDTH), BF16),
            pltpu.VMEM((2, d, BLOCK, GROUP_WIDTH), BF16),
            pltpu.VMEM((tm, GROUP_WIDTH), BF16),
            pltpu.VMEM((tm, 128), F32),
        ],
        compiler_params=pltpu.CompilerParams(
            dimension_semantics=("arbitrary", "arbitrary"),
            vmem_limit_bytes=56 * MIB),
        name=f"attn_d{d}",
    )(rel_bias_g, bucket, pm, pmt, qkv, qkv, qkv, *casts)


def _mix_out_kernel(tiles_per_seq,
                    u_ref, halo_ref, pw_ref, ps_ref,
                    o1_ref, o2_ref, o3_ref, l1_ref, l2_ref, l3_ref,
                    wo_ref, x_ref, g_ref,
                    h_ref, hn_ref, ext_ref, a_ref):
    i = pl.program_id(0)
    tm = u_ref.shape[0]
    seq_tile = i % tiles_per_seq

    halo = halo_ref[...]
    ext_ref[0:POOL_HALO, :] = jnp.where(seq_tile > 0, halo, jnp.zeros_like(halo))
    ext_ref[POOL_HALO:, :] = u_ref[...]
    pos = seq_tile * tm + lax.broadcasted_iota(jnp.int32, (tm, 1), 0)
    for gi, w in enumerate(POOL_WINDOWS):
        cs = slice(gi * POOL_GROUP_DIM, (gi + 1) * POOL_GROUP_DIM)
        cur = ext_ref[POOL_HALO:, cs]
        tot = cur
        for k in range(1, w):
            tot = tot + ext_ref[POOL_HALO - k:POOL_HALO - k + tm, cs]
        cnt = jnp.minimum(pos + 1, w).astype(F32)
        pooled = tot / cnt - cur
        mixed = jnp.dot(pooled.astype(BF16), pw_ref[gi], preferred_element_type=F32)
        a_ref[:, cs] = (mixed * ps_ref[:, cs]).astype(BF16)

    l1, l2, l3 = l1_ref[...], l2_ref[...], l3_ref[...]
    m = jnp.maximum(jnp.maximum(l1, l2), l3)
    e1, e2, e3 = jnp.exp(l1 - m), jnp.exp(l2 - m), jnp.exp(l3 - m)
    tot_e = e1 + e2 + e3
    for gi, (o_ref, e) in enumerate(((o1_ref, e1), (o2_ref, e2), (o3_ref, e3))):
        alpha = e / tot_e
        for h in range(HEADS_PER_GROUP):
            a_h = alpha[:, h * LSE_LANES:h * LSE_LANES + 1]
            hs = slice(h * HEAD_DIM, (h + 1) * HEAD_DIM)
            col = POOL_WIDTH + gi * GROUP_WIDTH + h * HEAD_DIM
            a_ref[:, col:col + HEAD_DIM] = (o_ref[:, hs].astype(F32) * a_h).astype(BF16)

    h = x_ref[...] + jnp.dot(a_ref[...], wo_ref[...], preferred_element_type=F32)
    h_ref[...] = h
    hn_ref[...] = (h * _rms_scale(h) * g_ref[...]).astype(BF16)


def _mix_out(u, pool_w, pool_scale, outs, lses, w_out, x2, g_ffn, seq):
    T = x2.shape[0]
    tm = TM_MIX
    assert seq % tm == 0 and tm % POOL_HALO == 0
    row = lambda w: pl.BlockSpec((tm, w), lambda i: (i, 0))
    const = lambda shape: pl.BlockSpec(shape, lambda i: (0,) * len(shape))
    halo_blocks = tm // POOL_HALO
    return pl.pallas_call(
        functools.partial(_mix_out_kernel, seq // tm),
        grid=(T // tm,),
        in_specs=[
            row(POOL_WIDTH),
            pl.BlockSpec((POOL_HALO, POOL_WIDTH),
                         lambda i: (jnp.maximum(i * halo_blocks - 1, 0), 0)),
            const((len(POOL_WINDOWS), POOL_GROUP_DIM, POOL_GROUP_DIM)),
            const((1, POOL_WIDTH)),
            row(GROUP_WIDTH), row(GROUP_WIDTH), row(GROUP_WIDTH),
            row(128), row(128), row(128),
            pl.BlockSpec((D_MODEL, D_MODEL), lambda i: (0, 0), pipeline_mode=pl.Buffered(1)),
            row(D_MODEL),
            const((1, D_MODEL)),
        ],
        out_specs=[row(D_MODEL), row(D_MODEL)],
        out_shape=[
            jax.ShapeDtypeStruct((T, D_MODEL), F32),
            jax.ShapeDtypeStruct((T, D_MODEL), BF16),
        ],
        scratch_shapes=[
            pltpu.VMEM((POOL_HALO + tm, POOL_WIDTH), F32),
            pltpu.VMEM((tm, D_MODEL), BF16),
        ],
        compiler_params=pltpu.CompilerParams(
            dimension_semantics=("arbitrary",),
            vmem_limit_bytes=52 * MIB),
        name="mix_out",
    )(u, u, pool_w, pool_scale, *outs, *lses, w_out, x2, g_ffn)


def _ffn_kernel(nf, last, hn_ref, wg_ref, wu_ref, wd_ref, o_ref, a_ref):
    s = pl.program_id(0)
    chunk_d = lax.rem(jnp.maximum(s - 1, 0), nf)
    mid = (s > 0) & (s < last)

    def gate_up():
        hn = hn_ref[...]
        g = jnp.dot(hn, wg_ref[...], preferred_element_type=F32)
        u = jnp.dot(hn, wu_ref[...], preferred_element_type=F32)
        a_ref[s % 2] = (g * jax.nn.sigmoid(g) * u).astype(BF16)

    def down():
        return jnp.dot(a_ref[(s - 1) % 2], wd_ref[...], preferred_element_type=F32)

    @pl.when(s == 0)
    def _():
        gate_up()

    @pl.when(mid & (chunk_d == 0))
    def _():
        gate_up()
        o_ref[...] = down()

    @pl.when(mid & (chunk_d > 0))
    def _():
        gate_up()
        o_ref[...] += down()

    @pl.when(s == last)
    def _():
        o_ref[...] += down()


def _ffn(hn, w_gate, w_up, w_down):
    T = hn.shape[0]
    tm, tf = TM_FFN, TF_FFN
    nf = D_FF // tf
    last = (T // tm) * nf
    up_step = lambda s: jnp.minimum(s, last - 1)
    down_step = lambda s: jnp.maximum(s - 1, 0)
    return pl.pallas_call(
        functools.partial(_ffn_kernel, nf, last),
        grid=(last + 1,),
        in_specs=[
            pl.BlockSpec((tm, D_MODEL), lambda s: (lax.div(up_step(s), nf), 0)),
            pl.BlockSpec((D_MODEL, tf), lambda s: (0, lax.rem(up_step(s), nf))),
            pl.BlockSpec((D_MODEL, tf), lambda s: (0, lax.rem(up_step(s), nf))),
            pl.BlockSpec((tf, D_MODEL), lambda s: (lax.rem(down_step(s), nf), 0)),
        ],
        out_specs=pl.BlockSpec((tm, D_MODEL), lambda s: (lax.div(down_step(s), nf), 0)),
        out_shape=jax.ShapeDtypeStruct((T, D_MODEL), F32),
        scratch_shapes=[pltpu.VMEM((2, tm, tf), BF16)],
        compiler_params=pltpu.CompilerParams(
            dimension_semantics=("arbitrary",),
            vmem_limit_bytes=56 * MIB),
        name="ffn",
    )(hn, w_gate, w_up, w_down)


def _ple_kernel(h_ref, f_ref, p_ref, gp_ref, wpg_ref, wpp_ref, gf_ref, y_ref):
    for c in range(h_ref.shape[0] // SUB_PLE):
        rows = pl.ds(c * SUB_PLE, SUB_PLE)
        h = h_ref[rows, :] + f_ref[rows, :]
        hn = (h * _rms_scale(h) * gp_ref[...]).astype(BF16)
        gate = jax.nn.sigmoid(jnp.dot(hn, wpg_ref[...], preferred_element_type=F32))
        emb = jnp.dot(p_ref[rows, :].astype(BF16), wpp_ref[...], preferred_element_type=F32)
        h = h + gate * emb
        y_ref[rows, :] = h * _rms_scale(h) * gf_ref[...]


def _ple_final(h1, ffn, p2, g_ple, w_pg, w_pp, g_final):
    T = h1.shape[0]
    tm = TM_PLE
    return pl.pallas_call(
        _ple_kernel,
        grid=(T // tm,),
        in_specs=[
            pl.BlockSpec((tm, D_MODEL), lambda i: (i, 0)),
            pl.BlockSpec((tm, D_MODEL), lambda i: (i, 0)),
            pl.BlockSpec((tm, D_PLE), lambda i: (i, 0)),
            pl.BlockSpec((1, D_MODEL), lambda i: (0, 0)),
            pl.BlockSpec((D_MODEL, D_MODEL), lambda i: (0, 0), pipeline_mode=pl.Buffered(1)),
            pl.BlockSpec((D_PLE, D_MODEL), lambda i: (0, 0)),
            pl.BlockSpec((1, D_MODEL), lambda i: (0, 0)),
        ],
        out_specs=pl.BlockSpec((tm, D_MODEL), lambda i: (i, 0)),
        out_shape=jax.ShapeDtypeStruct((T, D_MODEL), F32),
        compiler_params=pltpu.CompilerParams(
            dimension_semantics=("arbitrary",),
            vmem_limit_bytes=48 * MIB),
        name="ple_final",
    )(h1, ffn, p2, g_ple, w_pg, w_pp, g_final)


def kernel(x, p, rel_bias, norm_mix_g, w_in, pool_w, pool_scale, w_out, norm_ffn_g,
           w_gate, w_up, w_down, norm_ple_g, w_ple_gate, w_ple_proj, final_norm_g):
    B, S, D = x.shape
    depth = w_in.shape[0]
    T = B * S
    h = x.reshape(T, D)
    row = lambda v: v.reshape(1, -1).astype(F32)
    for i in range(depth):
        u, qkv = _in_proj(h, row(norm_mix_g[i]), w_in[i].astype(BF16))
        casts = ((w_gate[i], w_out[i]), (w_up[i], w_ple_gate[i]), (w_down[i],))
        outs, lses, w16 = [], [], []
        for gi in range(len(DILATION_PAIRS)):
            hs = slice(gi * HEADS_PER_GROUP, (gi + 1) * HEADS_PER_GROUP)
            o, l, *w = _attention_group(qkv, rel_bias[:, hs].astype(F32), gi, B, S, casts[gi])
            outs.append(o)
            lses.append(l)
            w16 += w
        w_gate16, w_out16, w_up16, w_ple_gate16, w_down16 = w16
        h1, hn = _mix_out(u, pool_w[i].astype(BF16), row(pool_scale[i]), outs, lses,
                          w_out16, h, row(norm_ffn_g[i]), S)
        ffn = _ffn(hn, w_gate16, w_up16, w_down16)
        assert depth == 1
        h = _ple_final(h1, ffn, p[i].reshape(T, D_PLE), row(norm_ple_g[i]),
                       w_ple_gate16, w_ple_proj[i].astype(BF16), row(final_norm_g))
    return h.reshape(B, S, D)
```
